```python
import jax, jax.numpy as jnp
from jax import lax
import numpy as np

D_MODEL = 2048
BATCH = 2
SEQ = 16384
DEPTH = 4

N_META = 16
RET_HEADS = 8
RET_DK = D_MODEL // 16
RET_DV = 2 * RET_DK
RET_QK_W = RET_HEADS * RET_DK
RET_V_W = RET_HEADS * RET_DV
CONV_W = D_MODEL
CONV_K = 3
CHUNK = 128
ROPE_BASE = 10000.0
EPS = 1e-6
N_BRANCH = 2
IN_SIZES = (RET_QK_W, RET_QK_W, RET_V_W, RET_V_W, CONV_W, CONV_W, CONV_W, CONV_W, D_MODEL, D_MODEL)
IN_COLS = RET_QK_W * 2 + RET_V_W * 2 + CONV_W * 4 + D_MODEL * 2

kernel_name = "hybrid_retention_shortconv_gated_merge"


def rms_norm(x, g):
    xf = x.astype(jnp.float32)
    y = xf * lax.rsqrt(jnp.mean(xf * xf, axis=-1, keepdims=True) + EPS)
    return (y * g.astype(jnp.float32)).astype(x.dtype)


def split_columns(proj):
    points = []
    acc = 0
    for s in IN_SIZES[:-1]:
        acc += s
        points.append(acc)
    return jnp.split(proj, points, axis=-1)


def rotary(x, pos):
    half = x.shape[-1] // 2
    inv_freq = ROPE_BASE ** (-jnp.arange(half, dtype=jnp.float32) / half)
    ang = pos.astype(jnp.float32)[:, None] * inv_freq[None, :]
    cos = jnp.cos(ang)[None, :, None, :]
    sin = jnp.sin(ang)[None, :, None, :]
    xf = x.astype(jnp.float32)
    x1, x2 = xf[..., :half], xf[..., half:]
    out = jnp.concatenate([x1 * cos - x2 * sin, x1 * sin + x2 * cos], axis=-1)
    return out.astype(x.dtype)


def retention_chunkwise(q, k, v):
    B, L, H, dk = q.shape
    dv = v.shape[-1]
    pad = (-L) % CHUNK
    padw = ((0, 0), (pad, 0), (0, 0), (0, 0))
    q, k, v = [jnp.pad(t.astype(jnp.float32), padw) for t in (q, k, v)]
    n = (L + pad) // CHUNK

    def to_chunks(t):
        return t.reshape(B, n, CHUNK, H, t.shape[-1]).transpose(1, 0, 3, 2, 4)

    qc, kc, vc = to_chunks(q), to_chunks(k), to_chunks(v)
    log_g = jnp.log(1.0 - jnp.exp2(-5.0 - jnp.arange(H, dtype=jnp.float32)))
    idx = jnp.arange(CHUNK, dtype=jnp.float32)
    diff = idx[:, None] - idx[None, :]
    decay = jnp.where(diff >= 0, jnp.exp(log_g[:, None, None] * jnp.maximum(diff, 0.0)), 0.0)
    q_decay = jnp.exp(log_g[:, None] * (idx + 1.0))[None, :, :, None]
    k_decay = jnp.exp(log_g[:, None] * (CHUNK - 1.0 - idx))[None, :, :, None]
    chunk_decay = jnp.exp(log_g * CHUNK)[None, :, None, None]

    def step(state, xs):
        qi, ki, vi = xs
        scores = jnp.einsum('bhid,bhjd->bhij', qi, ki) * decay[None]
        inner = jnp.einsum('bhij,bhjv->bhiv', scores, vi)
        cross = jnp.einsum('bhid,bhdv->bhiv', qi, state) * q_decay
        new_state = state * chunk_decay + jnp.einsum('bhjd,bhjv->bhdv', ki * k_decay, vi)
        return new_state, inner + cross

    s0 = jnp.zeros((B, H, dk, dv), jnp.float32)
    _, out = lax.scan(step, s0, (qc, kc, vc))
    out = out.transpose(1, 0, 3, 2, 4).reshape(B, n * CHUNK, H, dv)
    return out[:, pad:]


def head_group_norm(y, g):
    B, L, H, dv = y.shape
    mu = jnp.mean(y, axis=-1, keepdims=True)
    yc = y - mu
    var = jnp.mean(yc * yc, axis=-1, keepdims=True)
    yn = (yc * lax.rsqrt(var + EPS)).reshape(B, L, H * dv)
    return yn * g.astype(jnp.float32)


def causal_depthwise_conv(u, w, b):
    C = u.shape[-1]
    y = lax.conv_general_dilated(
        u, w[:, None, :].astype(u.dtype), window_strides=(1,), padding=[(CONV_K - 1, 0)],
        dimension_numbers=('NWC', 'WIO', 'NWC'), feature_group_count=C)
    return y + b.astype(u.dtype)


def hybrid_layer(h, pos, norm_g, w_in, conv_w, conv_b, gn_g, w_branch, w_out):
    B, L, _ = h.shape
    u = rms_norm(h, norm_g)
    proj = jnp.einsum('bld,de->ble', u, w_in)
    q, k, v, g_ret, c_x, c_pre, c_post, g_conv, m_ret, m_conv = split_columns(proj)

    q = rotary(q.reshape(B, L, RET_HEADS, RET_DK), pos) * (RET_DK ** -0.5)
    k = rotary(k.reshape(B, L, RET_HEADS, RET_DK), pos)
    v = v.reshape(B, L, RET_HEADS, RET_DV)
    y_ret = head_group_norm(retention_chunkwise(q, k, v), gn_g)
    y_ret = (y_ret * jax.nn.silu(g_ret.astype(jnp.float32))).astype(h.dtype)
    y_ret = jnp.einsum('blc,cd->bld', y_ret, w_branch[0])

    z = causal_depthwise_conv(c_pre * c_x, conv_w, conv_b)
    y_conv = (c_post * z) * jax.nn.silu(g_conv)
    y_conv = jnp.einsum('blc,cd->bld', y_conv, w_branch[1])

    merged = jax.nn.sigmoid(m_ret) * y_ret + jax.nn.sigmoid(m_conv) * y_conv
    return h + jnp.einsum('bld,de->ble', merged, w_out)


def setup_inputs(seed: int = 0) -> dict:
    key = jax.random.key(seed)
    ks = jax.random.split(key, 10)
    f32 = jnp.float32
    x = jax.random.normal(ks[0], (BATCH, SEQ, D_MODEL), f32)
    meta_tokens = jax.random.normal(ks[1], (N_META, D_MODEL), f32)
    norm_g = 1.0 + 0.02 * jax.random.normal(ks[2], (DEPTH, D_MODEL), f32)
    w_in = jax.random.normal(ks[3], (DEPTH, D_MODEL, IN_COLS), f32) * (D_MODEL ** -0.5)
    conv_w = jax.random.normal(ks[4], (DEPTH, CONV_K, CONV_W), f32) * (CONV_K ** -0.5)
    conv_b = 0.02 * jax.random.normal(ks[5], (DEPTH, CONV_W), f32)
    gn_g = 1.0 + 0.02 * jax.random.normal(ks[6], (DEPTH, RET_V_W), f32)
    w_branch = jax.random.normal(ks[7], (DEPTH, N_BRANCH, RET_V_W, D_MODEL), f32) * (RET_V_W ** -0.5)
    w_out = jax.random.normal(ks[8], (DEPTH, D_MODEL, D_MODEL), f32) * (D_MODEL ** -0.5)
    final_norm_g = 1.0 + 0.02 * jax.random.normal(ks[9], (D_MODEL,), f32)
    return {"x": x, "meta_tokens": meta_tokens, "norm_g": norm_g, "w_in": w_in,
            "conv_w": conv_w, "conv_b": conv_b, "gn_g": gn_g, "w_branch": w_branch,
            "w_out": w_out, "final_norm_g": final_norm_g}


def reference(x, meta_tokens, norm_g, w_in, conv_w, conv_b, gn_g, w_branch, w_out, final_norm_g):
    B = x.shape[0]
    meta = jnp.broadcast_to(meta_tokens.astype(x.dtype)[None], (B, N_META, x.shape[-1]))
    h = jnp.concatenate([meta, x], axis=1)
    pos = jnp.arange(h.shape[1], dtype=jnp.int32)
    for layer in range(DEPTH):
        h = hybrid_layer(h, pos, norm_g[layer], w_in[layer], conv_w[layer], conv_b[layer],
                         gn_g[layer], w_branch[layer], w_out[layer])
    return rms_norm(h, final_norm_g)[:, N_META:]
```

```python
import functools

import jax
import jax.numpy as jnp
from jax import lax
from jax.experimental import pallas as pl
from jax.experimental.pallas import tpu as pltpu

F32 = jnp.float32
BF16 = jnp.bfloat16

N_META = 16
RET_HEADS = 8
CONV_K = 3
CHUNK = 128
ROPE_BASE = 10000.0
EPS = 1e-6

V7X_LANES = 128
V7X_BF16_SUBLANES = 16
V7X_VMEM_BYTES = 64 * 1024 * 1024

ROW_TILE_TARGET = 768
COL_TILE = 2048
EPI_ROWS = V7X_BF16_SUBLANES
VMEM_LIMIT = V7X_VMEM_BYTES - 8 * 1024 * 1024


def _row_tile(m, target):
    best = CHUNK
    for t in range(CHUNK, target + 1, CHUNK):
        if m % t == 0:
            best = t
    return best


def _params(*sem):
    return pltpu.CompilerParams(dimension_semantics=sem, vmem_limit_bytes=VMEM_LIMIT)


def _rows(c, n):
    return pl.ds(pl.multiple_of(c * n, n), n)


def _rmsnorm_to(u_ref, x_ref, g_ref):
    x = x_ref[...]
    ms = jnp.mean(x * x, axis=-1, keepdims=True)
    u_ref[...] = ((x * lax.rsqrt(ms + EPS)) * g_ref[...]).astype(BF16)


def _proj_main_kernel(x_ref, g_ref, w_ref, cos_ref, sin_ref, o_ref, u_ref, acc_ref, *,
                      tm, tn, dk, qk_w, n_silu_lo, n_silu_hi, q_scale):
    j = pl.program_id(1)

    @pl.when(j == 0)
    def _():
        _rmsnorm_to(u_ref, x_ref, g_ref)

    acc_ref[...] = jnp.dot(u_ref[...], w_ref[...], preferred_element_type=F32)
    n_chunks = tm // EPI_ROWS
    n_groups = tn // dk
    q_groups = qk_w // dk

    @pl.when(j == 0)
    def _():
        def body(c, carry):
            r = _rows(c, EPI_ROWS)
            cs = cos_ref[r, :]
            sn = sin_ref[r, :]
            for gidx in range(n_groups):
                cols = slice(gidx * dk, (gidx + 1) * dk)
                x = acc_ref[r, cols]
                y = x * cs + pltpu.roll(x, dk // 2, 1) * sn
                if gidx < q_groups:
                    y = y * q_scale
                o_ref[r, cols] = y.astype(BF16)
            return carry
        lax.fori_loop(0, n_chunks, body, 0)

    @pl.when(j == 1)
    def _():
        def body(c, carry):
            r = _rows(c, EPI_ROWS)
            o_ref[r, :] = acc_ref[r, :].astype(BF16)
            return carry
        lax.fori_loop(0, n_chunks, body, 0)

    @pl.when(jnp.logical_and(j >= n_silu_lo, j < n_silu_hi))
    def _():
        def body(c, carry):
            r = _rows(c, EPI_ROWS)
            o_ref[r, :] = jax.nn.silu(acc_ref[r, :]).astype(BF16)
            return carry
        lax.fori_loop(0, n_chunks, body, 0)

    @pl.when(j >= n_silu_hi)
    def _():
        def body(c, carry):
            r = _rows(c, EPI_ROWS)
            o_ref[r, :] = jax.nn.sigmoid(acc_ref[r, :]).astype(BF16)
            return carry
        lax.fori_loop(0, n_chunks, body, 0)


def _proj_main(h, norm_g, w, cosf, sinf, *, tm, dk, qk_w):
    m, d = h.shape
    n = w.shape[1]
    tn = COL_TILE
    assert 2 * qk_w == tn and n % tn == 0
    v_w = 2 * qk_w
    n_silu_lo = (2 * qk_w + v_w) // tn
    n_silu_hi = n_silu_lo + v_w // tn
    kern = functools.partial(_proj_main_kernel, tm=tm, tn=tn, dk=dk, qk_w=qk_w,
                             n_silu_lo=n_silu_lo, n_silu_hi=n_silu_hi, q_scale=dk ** -0.5)
    return pl.pallas_call(
        kern,
        grid=(m // tm, n // tn),
        in_specs=[
            pl.BlockSpec((tm, d), lambda i, j: (i, 0)),
            pl.BlockSpec((1, d), lambda i, j: (0, 0)),
            pl.BlockSpec((d, tn), lambda i, j: (0, j)),
            pl.BlockSpec((tm, dk), lambda i, j: (i, 0)),
            pl.BlockSpec((tm, dk), lambda i, j: (i, 0)),
        ],
        out_specs=pl.BlockSpec((tm, tn), lambda i, j: (i, j)),
        out_shape=jax.ShapeDtypeStruct((m, n), BF16),
        scratch_shapes=[pltpu.VMEM((tm, d), BF16), pltpu.VMEM((tm, tn), F32)],
        compiler_params=_params("parallel", "arbitrary"),
        name="proj_main",
    )(h, norm_g, w, cosf, sinf)


def _proj_conv_kernel(x_ref, g_ref, w_ref, cw_ref, cb_ref, o_ref, u_ref, acc_ref, p_ref,
                      carry_ref, *, tm, tc):
    i = pl.program_id(0)
    j = pl.program_id(1)
    halo = 8

    @pl.when(j == 0)
    def _():
        _rmsnorm_to(u_ref, x_ref, g_ref)

    @pl.when(i == 0)
    def _():
        carry_ref[j] = jnp.zeros((halo, tc), F32)

    acc_ref[...] = jnp.dot(u_ref[...], w_ref[...], preferred_element_type=F32)

    p_ref[0:halo, :] = carry_ref[j]
    n_chunks = tm // EPI_ROWS

    def pbody(c, carry):
        r = _rows(c, EPI_ROWS)
        rp = pl.ds(pl.multiple_of(c * EPI_ROWS + halo, 8), EPI_ROWS)
        p_ref[rp, :] = acc_ref[r, tc:2 * tc] * acc_ref[r, 0:tc]
        return carry
    lax.fori_loop(0, n_chunks, pbody, 0)
    carry_ref[j] = p_ref[tm:tm + halo, :]

    w0 = cw_ref[0:1, :]
    w1 = cw_ref[1:2, :]
    w2 = cw_ref[2:3, :]
    cb = cb_ref[...]
    for c in range(n_chunks):
        r0 = c * EPI_ROWS
        z = (w0 * p_ref[r0 + halo - 2:r0 + halo - 2 + EPI_ROWS, :]
             + w1 * p_ref[r0 + halo - 1:r0 + halo - 1 + EPI_ROWS, :]
             + w2 * p_ref[r0 + halo:r0 + halo + EPI_ROWS, :]) + cb
        c_post = acc_ref[r0:r0 + EPI_ROWS, 2 * tc:3 * tc]
        gate = jax.nn.silu(acc_ref[r0:r0 + EPI_ROWS, 3 * tc:4 * tc])
        o_ref[r0:r0 + EPI_ROWS, :] = ((c_post * z) * gate).astype(BF16)


def _proj_conv(h, norm_g, w, conv_w, conv_b, *, tm):
    m, d = h.shape
    n = w.shape[1]
    tn = COL_TILE
    tc = tn // 4
    nj = n // tn
    c_w = n // 4
    kern = functools.partial(_proj_conv_kernel, tm=tm, tc=tc)
    return pl.pallas_call(
        kern,
        grid=(m // tm, nj),
        in_specs=[
            pl.BlockSpec((tm, d), lambda i, j: (i, 0)),
            pl.BlockSpec((1, d), lambda i, j: (0, 0)),
            pl.BlockSpec((d, tn), lambda i, j: (0, j)),
            pl.BlockSpec((CONV_K, tc), lambda i, j: (0, j)),
            pl.BlockSpec((1, tc), lambda i, j: (0, j)),
        ],
        out_specs=pl.BlockSpec((tm, tc), lambda i, j: (i, j)),
        out_shape=jax.ShapeDtypeStruct((m, c_w), BF16),
        scratch_shapes=[pltpu.VMEM((tm, d), BF16), pltpu.VMEM((tm, tn), F32),
                        pltpu.VMEM((tm + 8, tc), F32), pltpu.VMEM((nj, 8, tc), F32)],
        compiler_params=_params("arbitrary", "arbitrary"),
        name="proj_conv",
    )(h, norm_g, w, conv_w, conv_b)


def _retention_kernel(cd_ref, q_ref, k_ref, v_ref, sg_ref, dec_ref, qd_ref, kd_ref, gn_ref,
                      o_ref, s_ref, *, tt, heads, dk, dv):
    @pl.when(pl.program_id(1) == 0)
    def _():
        s_ref[...] = jnp.zeros_like(s_ref)

    nt = (((1,), (1,)), ((), ()))
    tn = (((0,), (0,)), ((), ()))

    def chunk(c, carry):
        r = _rows(c, CHUNK)
        for hd in range(heads):
            kc = slice(hd * dk, (hd + 1) * dk)
            vc = slice(hd * dv, (hd + 1) * dv)
            qh = q_ref[r, kc]
            kh = k_ref[r, kc]
            vh = v_ref[r, vc]
            scores = lax.dot_general(qh, kh, nt, preferred_element_type=F32) * dec_ref[hd]
            inner = jnp.dot(scores.astype(BF16), vh, preferred_element_type=F32)
            state = s_ref[hd]
            cross = jnp.dot(qh, state.astype(BF16), preferred_element_type=F32) * qd_ref[hd]
            kdec = (kh.astype(F32) * kd_ref[hd]).astype(BF16)
            s_ref[hd] = state * cd_ref[hd] + lax.dot_general(kdec, vh, tn,
                                                             preferred_element_type=F32)
            y = inner + cross
            mu = jnp.mean(y, axis=-1, keepdims=True)
            yc = y - mu
            var = jnp.mean(yc * yc, axis=-1, keepdims=True)
            yn = (yc * lax.rsqrt(var + EPS)) * gn_ref[:, vc]
            o_ref[r, vc] = (yn * sg_ref[r, vc].astype(F32)).astype(BF16)
        return carry

    lax.fori_loop(0, tt // CHUNK, chunk, 0)


def _retention(p, gn_g, chunk_decay, decay, q_decay, k_decay, *, batch, lp, heads, dk, dv, tt):
    m = p.shape[0]
    qk_w = heads * dk
    v_w = heads * dv
    nt = lp // tt
    kern = functools.partial(_retention_kernel, tt=tt, heads=heads, dk=dk, dv=dv)
    row = lambda b, t: b * nt + t
    return pl.pallas_call(
        kern,
        grid=(batch, nt),
        in_specs=[
            pl.BlockSpec(memory_space=pltpu.SMEM),
            pl.BlockSpec((tt, qk_w), lambda b, t: (row(b, t), 0)),
            pl.BlockSpec((tt, qk_w), lambda b, t: (row(b, t), 1)),
            pl.BlockSpec((tt, v_w), lambda b, t: (row(b, t), 1)),
            pl.BlockSpec((tt, v_w), lambda b, t: (row(b, t), 2)),
            pl.BlockSpec((heads, CHUNK, CHUNK), lambda b, t: (0, 0, 0)),
            pl.BlockSpec((heads, CHUNK, dv), lambda b, t: (0, 0, 0)),
            pl.BlockSpec((heads, CHUNK, dk), lambda b, t: (0, 0, 0)),
            pl.BlockSpec((1, v_w), lambda b, t: (0, 0)),
        ],
        out_specs=pl.BlockSpec((tt, v_w), lambda b, t: (row(b, t), 0)),
        out_shape=jax.ShapeDtypeStruct((m, v_w), BF16),
        scratch_shapes=[pltpu.VMEM((heads, dk, dv), F32)],
        compiler_params=_params("arbitrary", "arbitrary"),
        name="retention",
    )(chunk_decay, p, p, p, p, decay, q_decay, k_decay, gn_g)


def _branch_kernel(yr_ref, yc_ref, w0_ref, w1_ref, mr_ref, mc_ref, o_ref):
    a = jnp.dot(yr_ref[...], w0_ref[...], preferred_element_type=F32)
    b = jnp.dot(yc_ref[...], w1_ref[...], preferred_element_type=F32)
    o_ref[...] = (mr_ref[...].astype(F32) * a + mc_ref[...].astype(F32) * b).astype(BF16)


def _branch(yr, yc, w0, w1, p, *, tm, gate_col0):
    m, kdim = yr.shape
    n = w0.shape[1]
    tn = COL_TILE
    nj = n // tn
    g0 = gate_col0 // tn
    return pl.pallas_call(
        _branch_kernel,
        grid=(m // tm, nj),
        in_specs=[
            pl.BlockSpec((tm, kdim), lambda i, j: (i, 0)),
            pl.BlockSpec((tm, kdim), lambda i, j: (i, 0)),
            pl.BlockSpec((kdim, tn), lambda i, j: (0, j)),
            pl.BlockSpec((kdim, tn), lambda i, j: (0, j)),
            pl.BlockSpec((tm, tn), lambda i, j: (i, g0 + j)),
            pl.BlockSpec((tm, tn), lambda i, j: (i, g0 + nj + j)),
        ],
        out_specs=pl.BlockSpec((tm, tn), lambda i, j: (i, j)),
        out_shape=jax.ShapeDtypeStruct((m, n), BF16),
        compiler_params=_params("parallel", "arbitrary"),
        name="branch",
    )(yr, yc, w0, w1, p, p)


def _out_kernel(a_ref, w_ref, h_ref, o_ref):
    o_ref[...] = h_ref[...] + jnp.dot(a_ref[...], w_ref[...], preferred_element_type=F32)


def _out_proj(a, w, h, *, tm):
    m, kdim = a.shape
    n = w.shape[1]
    tn = COL_TILE
    return pl.pallas_call(
        _out_kernel,
        grid=(m // tm, n // tn),
        in_specs=[
            pl.BlockSpec((tm, kdim), lambda i, j: (i, 0)),
            pl.BlockSpec((kdim, tn), lambda i, j: (0, j)),
            pl.BlockSpec((tm, tn), lambda i, j: (i, j)),
        ],
        out_specs=pl.BlockSpec((tm, tn), lambda i, j: (i, j)),
        out_shape=jax.ShapeDtypeStruct((m, n), F32),
        compiler_params=_params("parallel", "arbitrary"),
        name="out_proj",
    )(a, w, h)


def _final_norm_kernel(x_ref, g_ref, o_ref):
    x = x_ref[0]
    ms = jnp.mean(x * x, axis=-1, keepdims=True)
    o_ref[0] = (x * lax.rsqrt(ms + EPS)) * g_ref[...]


def _final_norm(h3, g, *, seq, tr):
    b, lp, d = h3.shape
    skip = (lp - seq) // tr
    return pl.pallas_call(
        _final_norm_kernel,
        grid=(b, seq // tr),
        in_specs=[
            pl.BlockSpec((1, tr, d), lambda bi, t: (bi, t + skip, 0)),
            pl.BlockSpec((1, d), lambda bi, t: (0, 0)),
        ],
        out_specs=pl.BlockSpec((1, tr, d), lambda bi, t: (bi, t, 0)),
        out_shape=jax.ShapeDtypeStruct((b, seq, d), F32),
        compiler_params=_params("parallel", "parallel"),
        name="final_norm",
    )(h3, g)


def _tables(lp, front, batch, dk, dv, heads):
    half = dk // 2
    inv_freq = ROPE_BASE ** (-jnp.arange(half, dtype=F32) / half)
    pos = jnp.maximum(jnp.arange(lp, dtype=jnp.int32) - front, 0)
    ang = pos.astype(F32)[:, None] * inv_freq[None, :]
    cos = jnp.cos(ang)
    sin = jnp.sin(ang)
    cosf = jnp.tile(jnp.concatenate([cos, cos], axis=-1), (batch, 1))
    sinf = jnp.tile(jnp.concatenate([-sin, sin], axis=-1), (batch, 1))

    log_g = jnp.log(1.0 - jnp.exp2(-5.0 - jnp.arange(heads, dtype=F32)))
    idx = jnp.arange(CHUNK, dtype=F32)
    diff = idx[:, None] - idx[None, :]
    decay = jnp.where(diff >= 0, jnp.exp(log_g[:, None, None] * jnp.maximum(diff, 0.0)), 0.0)
    q_decay = jnp.broadcast_to(jnp.exp(log_g[:, None] * (idx + 1.0))[:, :, None],
                               (heads, CHUNK, dv))
    k_decay = jnp.broadcast_to(jnp.exp(log_g[:, None] * (CHUNK - 1.0 - idx))[:, :, None],
                               (heads, CHUNK, dk))
    chunk_decay = jnp.exp(log_g * CHUNK)
    return cosf, sinf, decay, q_decay, k_decay, chunk_decay


def kernel(x, meta_tokens, norm_g, w_in, conv_w, conv_b, gn_g, w_branch, w_out, final_norm_g):
    batch, seq, d = x.shape
    depth = w_in.shape[0]
    heads = RET_HEADS
    dk = d // 16
    dv = 2 * dk
    qk_w = heads * dk
    v_w = heads * dv
    c_w = d
    assert dk == V7X_LANES and seq % CHUNK == 0

    front = CHUNK - N_META
    lp = front + N_META + seq
    m = batch * lp
    tm = _row_tile(m, ROW_TILE_TARGET)
    tt = _row_tile(lp, 3 * CHUNK)

    meta = jnp.broadcast_to(meta_tokens.astype(x.dtype)[None], (batch, N_META, d))
    h = jnp.concatenate([jnp.zeros((batch, front, d), x.dtype), meta, x], axis=1).reshape(m, d)

    cosf, sinf, decay, q_decay, k_decay, chunk_decay = _tables(lp, front, batch, dk, dv, heads)

    o_q, o_k, o_v = 0, qk_w, 2 * qk_w
    o_gr = o_v + v_w
    o_cx = o_gr + v_w
    o_cpre, o_cpost, o_gc = o_cx + c_w, o_cx + 2 * c_w, o_cx + 3 * c_w
    o_mr = o_cx + 4 * c_w
    tc = COL_TILE // 4

    for layer in range(depth):
        w = w_in[layer]
        w_main = jnp.concatenate([w[:, o_q:o_cx], w[:, o_mr:]], axis=1).astype(BF16)
        conv_cols = []
        for jb in range(c_w // tc):
            for off in (o_cx, o_cpre, o_cpost, o_gc):
                conv_cols.append(w[:, off + jb * tc: off + (jb + 1) * tc])
        w_conv = jnp.concatenate(conv_cols, axis=1).astype(BF16)
        ng = norm_g[layer][None, :]

        p = _proj_main(h, ng, w_main, cosf, sinf, tm=tm, dk=dk, qk_w=qk_w)
        yc = _proj_conv(h, ng, w_conv, conv_w[layer], conv_b[layer][None, :], tm=tm)
        yr = _retention(p, gn_g[layer][None, :], chunk_decay, decay, q_decay, k_decay,
                        batch=batch, lp=lp, heads=heads, dk=dk, dv=dv, tt=tt)
        merged = _branch(yr, yc, w_branch[layer, 0].astype(BF16), w_branch[layer, 1].astype(BF16),
                         p, tm=tm, gate_col0=2 * qk_w + 2 * v_w)
        h = _out_proj(merged, w_out[layer].astype(BF16), h, tm=tm)

    return _final_norm(h.reshape(batch, lp, d), final_norm_g[None, :], seq=seq, tr=CHUNK)
```

```python
import functools

import jax
import jax.numpy as jnp
from jax import lax
from jax.experimental import pallas as pl
from jax.experimental.pallas import tpu as pltpu

F32 = jnp.float32
BF16 = jnp.bfloat16

N_META = 16
RET_HEADS = 8
CONV_K = 3
CHUNK = 128
ROPE_BASE = 10000.0
EPS = 1e-6

V7X_LANES = 128
V7X_BF16_SUBLANES = 16
V7X_VMEM_BYTES = 64 * 1024 * 1024

ROW_TILE_TARGET = 768
COL_TILE = 2048
EPI_ROWS = V7X_BF16_SUBLANES
VMEM_LIMIT = V7X_VMEM_BYTES - 8 * 1024 * 1024


def _row_tile(m, target):
    best = CHUNK
    for t in range(CHUNK, target + 1, CHUNK):
        if m % t == 0:
            best = t
    return best


def _params(*sem):
    return pltpu.CompilerParams(dimension_semantics=sem, vmem_limit_bytes=VMEM_LIMIT)


def _rows(c, n):
    return pl.ds(pl.multiple_of(c * n, n), n)


def _rmsnorm_to(u_ref, x_ref, g_ref):
    x = x_ref[...]
    ms = jnp.mean(x * x, axis=-1, keepdims=True)
    u_ref[...] = ((x * lax.rsqrt(ms + EPS)) * g_ref[...]).astype(BF16)


def _proj_main_kernel(x_ref, g_ref, w_ref, cos_ref, sin_ref, o_ref, u_ref, *,
                      tn, dk, qk_w, n_silu_lo, n_silu_hi, q_scale):
    j = pl.program_id(1)
    n_groups = tn // dk
    q_groups = qk_w // dk

    def matmul():
        return jnp.dot(u_ref[...], w_ref[...], preferred_element_type=F32)

    @pl.when(j == 0)
    def _():
        _rmsnorm_to(u_ref, x_ref, g_ref)
        acc = matmul()
        cs = cos_ref[...]
        sn = sin_ref[...]
        for gidx in range(n_groups):
            cols = slice(gidx * dk, (gidx + 1) * dk)
            x = acc[:, cols]
            y = x * cs + pltpu.roll(x, dk // 2, 1) * sn
            if gidx < q_groups:
                y = y * q_scale
            o_ref[:, cols] = y.astype(BF16)

    @pl.when(j == 1)
    def _():
        o_ref[...] = matmul().astype(BF16)

    @pl.when(jnp.logical_and(j >= n_silu_lo, j < n_silu_hi))
    def _():
        o_ref[...] = jax.nn.silu(matmul()).astype(BF16)

    @pl.when(j >= n_silu_hi)
    def _():
        o_ref[...] = jax.nn.sigmoid(matmul()).astype(BF16)


def _proj_main(h, norm_g, w, cosf, sinf, *, tm, dk, qk_w):
    m, d = h.shape
    n = w.shape[1]
    tn = COL_TILE
    assert 2 * qk_w == tn and n % tn == 0
    v_w = 2 * qk_w
    n_silu_lo = (2 * qk_w + v_w) // tn
    n_silu_hi = n_silu_lo + v_w // tn
    kern = functools.partial(_proj_main_kernel, tn=tn, dk=dk, qk_w=qk_w,
                             n_silu_lo=n_silu_lo, n_silu_hi=n_silu_hi, q_scale=dk ** -0.5)
    return pl.pallas_call(
        kern,
        grid=(m // tm, n // tn),
        in_specs=[
            pl.BlockSpec((tm, d), lambda i, j: (i, 0)),
            pl.BlockSpec((1, d), lambda i, j: (0, 0)),
            pl.BlockSpec((d, tn), lambda i, j: (0, j)),
            pl.BlockSpec((tm, dk), lambda i, j: (i, 0)),
            pl.BlockSpec((tm, dk), lambda i, j: (i, 0)),
        ],
        out_specs=pl.BlockSpec((tm, tn), lambda i, j: (i, j)),
        out_shape=jax.ShapeDtypeStruct((m, n), BF16),
        scratch_shapes=[pltpu.VMEM((tm, d), BF16)],
        compiler_params=_params("parallel", "arbitrary"),
        name="proj_main",
    )(h, norm_g, w, cosf, sinf)


def _proj_conv_kernel(x_ref, g_ref, w_ref, cw_ref, cb_ref, o_ref, u_ref, p_ref, carry_ref, *,
                      tm, tc):
    i = pl.program_id(0)
    j = pl.program_id(1)
    halo = 8

    @pl.when(j == 0)
    def _():
        _rmsnorm_to(u_ref, x_ref, g_ref)

    @pl.when(i == 0)
    def _():
        carry_ref[j] = jnp.zeros((halo, tc), F32)

    acc = jnp.dot(u_ref[...], w_ref[...], preferred_element_type=F32)

    p = acc[:, tc:2 * tc] * acc[:, 0:tc]
    p_ref[0:halo, :] = carry_ref[j]
    p_ref[halo:halo + tm, :] = p
    carry_ref[j] = p[tm - halo:tm, :]
    z = (cw_ref[0:1, :] * p_ref[halo - 2:halo - 2 + tm, :]
         + cw_ref[1:2, :] * p_ref[halo - 1:halo - 1 + tm, :]
         + cw_ref[2:3, :] * p) + cb_ref[...]
    gate = jax.nn.silu(acc[:, 3 * tc:4 * tc])
    o_ref[...] = ((acc[:, 2 * tc:3 * tc] * z) * gate).astype(BF16)


def _proj_conv(h, norm_g, w, conv_w, conv_b, *, tm):
    m, d = h.shape
    n = w.shape[1]
    tn = COL_TILE
    tc = tn // 4
    nj = n // tn
    c_w = n // 4
    kern = functools.partial(_proj_conv_kernel, tm=tm, tc=tc)
    return pl.pallas_call(
        kern,
        grid=(m // tm, nj),
        in_specs=[
            pl.BlockSpec((tm, d), lambda i, j: (i, 0)),
            pl.BlockSpec((1, d), lambda i, j: (0, 0)),
            pl.BlockSpec((d, tn), lambda i, j: (0, j)),
            pl.BlockSpec((CONV_K, tc), lambda i, j: (0, j)),
            pl.BlockSpec((1, tc), lambda i, j: (0, j)),
        ],
        out_specs=pl.BlockSpec((tm, tc), lambda i, j: (i, j)),
        out_shape=jax.ShapeDtypeStruct((m, c_w), BF16),
        scratch_shapes=[pltpu.VMEM((tm, d), BF16), pltpu.VMEM((tm + 8, tc), F32),
                        pltpu.VMEM((nj, 8, tc), F32)],
        compiler_params=_params("arbitrary", "arbitrary"),
        name="proj_conv",
    )(h, norm_g, w, conv_w, conv_b)


def _retention_kernel(cd_ref, q_ref, k_ref, v_ref, sg_ref, dec_ref, qd_ref, kd_ref, gn_ref,
                      o_ref, s_ref, *, tt, heads, dk, dv):
    @pl.when(pl.program_id(1) == 0)
    def _():
        s_ref[...] = jnp.zeros_like(s_ref)

    nt = (((1,), (1,)), ((), ()))
    tn = (((0,), (0,)), ((), ()))

    def chunk(c, carry):
        r = _rows(c, CHUNK)
        for hd in range(heads):
            kc = slice(hd * dk, (hd + 1) * dk)
            vc = slice(hd * dv, (hd + 1) * dv)
            qh = q_ref[r, kc]
            kh = k_ref[r, kc]
            vh = v_ref[r, vc]
            scores = lax.dot_general(qh, kh, nt, preferred_element_type=F32) * dec_ref[hd]
            inner = jnp.dot(scores.astype(BF16), vh, preferred_element_type=F32)
            state = s_ref[hd]
            cross = jnp.dot(qh, state.astype(BF16), preferred_element_type=F32) * qd_ref[hd]
            kdec = (kh.astype(F32) * kd_ref[hd]).astype(BF16)
            s_ref[hd] = state * cd_ref[hd] + lax.dot_general(kdec, vh, tn,
                                                             preferred_element_type=F32)
            y = inner + cross
            mu = jnp.mean(y, axis=-1, keepdims=True)
            yc = y - mu
            var = jnp.mean(yc * yc, axis=-1, keepdims=True)
            yn = (yc * lax.rsqrt(var + EPS)) * gn_ref[:, vc]
            o_ref[r, vc] = (yn * sg_ref[r, vc].astype(F32)).astype(BF16)
        return carry

    lax.fori_loop(0, tt // CHUNK, chunk, 0)


def _retention(p, gn_g, chunk_decay, decay, q_decay, k_decay, *, batch, lp, heads, dk, dv, tt):
    m = p.shape[0]
    qk_w = heads * dk
    v_w = heads * dv
    nt = lp // tt
    kern = functools.partial(_retention_kernel, tt=tt, heads=heads, dk=dk, dv=dv)
    row = lambda b, t: b * nt + t
    return pl.pallas_call(
        kern,
        grid=(batch, nt),
        in_specs=[
            pl.BlockSpec(memory_space=pltpu.SMEM),
            pl.BlockSpec((tt, qk_w), lambda b, t: (row(b, t), 0)),
            pl.BlockSpec((tt, qk_w), lambda b, t: (row(b, t), 1)),
            pl.BlockSpec((tt, v_w), lambda b, t: (row(b, t), 1)),
            pl.BlockSpec((tt, v_w), lambda b, t: (row(b, t), 2)),
            pl.BlockSpec((heads, CHUNK, CHUNK), lambda b, t: (0, 0, 0)),
            pl.BlockSpec((heads, CHUNK, dv), lambda b, t: (0, 0, 0)),
            pl.BlockSpec((heads, CHUNK, dk), lambda b, t: (0, 0, 0)),
            pl.BlockSpec((1, v_w), lambda b, t: (0, 0)),
        ],
        out_specs=pl.BlockSpec((tt, v_w), lambda b, t: (row(b, t), 0)),
        out_shape=jax.ShapeDtypeStruct((m, v_w), BF16),
        scratch_shapes=[pltpu.VMEM((heads, dk, dv), F32)],
        compiler_params=_params("arbitrary", "arbitrary"),
        name="retention",
    )(chunk_decay, p, p, p, p, decay, q_decay, k_decay, gn_g)


def _branch_kernel(yr_ref, yc_ref, w0_ref, w1_ref, mr_ref, mc_ref, o_ref):
    a = jnp.dot(yr_ref[...], w0_ref[...], preferred_element_type=F32)
    b = jnp.dot(yc_ref[...], w1_ref[...], preferred_element_type=F32)
    o_ref[...] = (mr_ref[...].astype(F32) * a + mc_ref[...].astype(F32) * b).astype(BF16)


def _branch(yr, yc, w0, w1, p, *, tm, gate_col0):
    m, kdim = yr.shape
    n = w0.shape[1]
    tn = COL_TILE
    nj = n // tn
    g0 = gate_col0 // tn
    return pl.pallas_call(
        _branch_kernel,
        grid=(m // tm, nj),
        in_specs=[
            pl.BlockSpec((tm, kdim), lambda i, j: (i, 0)),
            pl.BlockSpec((tm, kdim), lambda i, j: (i, 0)),
            pl.BlockSpec((kdim, tn), lambda i, j: (0, j)),
            pl.BlockSpec((kdim, tn), lambda i, j: (0, j)),
            pl.BlockSpec((tm, tn), lambda i, j: (i, g0 + j)),
            pl.BlockSpec((tm, tn), lambda i, j: (i, g0 + nj + j)),
        ],
        out_specs=pl.BlockSpec((tm, tn), lambda i, j: (i, j)),
        out_shape=jax.ShapeDtypeStruct((m, n), BF16),
        compiler_params=_params("parallel", "arbitrary"),
        name="branch",
    )(yr, yc, w0, w1, p, p)


def _out_kernel(a_ref, w_ref, h_ref, o_ref):
    o_ref[...] = h_ref[...] + jnp.dot(a_ref[...], w_ref[...], preferred_element_type=F32)


def _out_proj(a, w, h, *, tm):
    m, kdim = a.shape
    n = w.shape[1]
    tn = COL_TILE
    return pl.pallas_call(
        _out_kernel,
        grid=(m // tm, n // tn),
        in_specs=[
            pl.BlockSpec((tm, kdim), lambda i, j: (i, 0)),
            pl.BlockSpec((kdim, tn), lambda i, j: (0, j)),
            pl.BlockSpec((tm, tn), lambda i, j: (i, j)),
        ],
        out_specs=pl.BlockSpec((tm, tn), lambda i, j: (i, j)),
        out_shape=jax.ShapeDtypeStruct((m, n), F32),
        compiler_params=_params("parallel", "arbitrary"),
        name="out_proj",
    )(a, w, h)


def _final_norm_kernel(x_ref, g_ref, o_ref):
    x = x_ref[0]
    ms = jnp.mean(x * x, axis=-1, keepdims=True)
    o_ref[0] = (x * lax.rsqrt(ms + EPS)) * g_ref[...]


def _final_norm(h3, g, *, seq, tr):
    b, lp, d = h3.shape
    skip = (lp - seq) // tr
    return pl.pallas_call(
        _final_norm_kernel,
        grid=(b, seq // tr),
        in_specs=[
            pl.BlockSpec((1, tr, d), lambda bi, t: (bi, t + skip, 0)),
            pl.BlockSpec((1, d), lambda bi, t: (0, 0)),
        ],
        out_specs=pl.BlockSpec((1, tr, d), lambda bi, t: (bi, t, 0)),
        out_shape=jax.ShapeDtypeStruct((b, seq, d), F32),
        compiler_params=_params("parallel", "parallel"),
        name="final_norm",
    )(h3, g)


def _tables(lp, front, batch, dk, dv, heads):
    half = dk // 2
    inv_freq = ROPE_BASE ** (-jnp.arange(half, dtype=F32) / half)
    pos = jnp.maximum(jnp.arange(lp, dtype=jnp.int32) - front, 0)
    ang = pos.astype(F32)[:, None] * inv_freq[None, :]
    cos = jnp.cos(ang)
    sin = jnp.sin(ang)
    cosf = jnp.tile(jnp.concatenate([cos, cos], axis=-1), (batch, 1))
    sinf = jnp.tile(jnp.concatenate([-sin, sin], axis=-1), (batch, 1))

    log_g = jnp.log(1.0 - jnp.exp2(-5.0 - jnp.arange(heads, dtype=F32)))
    idx = jnp.arange(CHUNK, dtype=F32)
    diff = idx[:, None] - idx[None, :]
    decay = jnp.where(diff >= 0, jnp.exp(log_g[:, None, None] * jnp.maximum(diff, 0.0)), 0.0)
    q_decay = jnp.broadcast_to(jnp.exp(log_g[:, None] * (idx + 1.0))[:, :, None],
                               (heads, CHUNK, dv))
    k_decay = jnp.broadcast_to(jnp.exp(log_g[:, None] * (CHUNK - 1.0 - idx))[:, :, None],
                               (heads, CHUNK, dk))
    chunk_decay = jnp.exp(log_g * CHUNK)
    return cosf, sinf, decay, q_decay, k_decay, chunk_decay


def kernel(x, meta_tokens, norm_g, w_in, conv_w, conv_b, gn_g, w_branch, w_out, final_norm_g):
    batch, seq, d = x.shape
    depth = w_in.shape[0]
    heads = RET_HEADS
    dk = d // 16
    dv = 2 * dk
    qk_w = heads * dk
    v_w = heads * dv
    c_w = d
    assert dk == V7X_LANES and seq % CHUNK == 0

    front = CHUNK - N_META
    lp = front + N_META + seq
    m = batch * lp
    tm = _row_tile(m, ROW_TILE_TARGET)
    tt = _row_tile(lp, 3 * CHUNK)

    meta = jnp.broadcast_to(meta_tokens.astype(x.dtype)[None], (batch, N_META, d))
    h = jnp.concatenate([jnp.zeros((batch, front, d), x.dtype), meta, x], axis=1).reshape(m, d)

    cosf, sinf, decay, q_decay, k_decay, chunk_decay = _tables(lp, front, batch, dk, dv, heads)

    o_q, o_k, o_v = 0, qk_w, 2 * qk_w
    o_gr = o_v + v_w
    o_cx = o_gr + v_w
    o_cpre, o_cpost, o_gc = o_cx + c_w, o_cx + 2 * c_w, o_cx + 3 * c_w
    o_mr = o_cx + 4 * c_w
    tc = COL_TILE // 4

    for layer in range(depth):
        w = w_in[layer]
        w_main = jnp.concatenate([w[:, o_q:o_cx], w[:, o_mr:]], axis=1).astype(BF16)
        conv_cols = []
        for jb in range(c_w // tc):
            for off in (o_cx, o_cpre, o_cpost, o_gc):
                conv_cols.append(w[:, off + jb * tc: off + (jb + 1) * tc])
        w_conv = jnp.concatenate(conv_cols, axis=1).astype(BF16)
        ng = norm_g[layer][None, :]

        p = _proj_main(h, ng, w_main, cosf, sinf, tm=tm, dk=dk, qk_w=qk_w)
        yc = _proj_conv(h, ng, w_conv, conv_w[layer], conv_b[layer][None, :], tm=tm)
        yr = _retention(p, gn_g[layer][None, :], chunk_decay, decay, q_decay, k_decay,
                        batch=batch, lp=lp, heads=heads, dk=dk, dv=dv, tt=tt)
        merged = _branch(yr, yc, w_branch[layer, 0].astype(BF16), w_branch[layer, 1].astype(BF16),
                         p, tm=tm, gate_col0=2 * qk_w + 2 * v_w)
        h = _out_proj(merged, w_out[layer].astype(BF16), h, tm=tm)

    return _final_norm(h.reshape(batch, lp, d), final_norm_g[None, :], seq=seq, tr=CHUNK)
```

```python
import functools

import numpy as np
import jax
import jax.numpy as jnp
from jax import lax
from jax.experimental import pallas as pl
from jax.experimental.pallas import tpu as pltpu

F32 = jnp.float32
BF16 = jnp.bfloat16

N_META = 16
RET_HEADS = 8
CONV_K = 3
CHUNK = 128
ROPE_BASE = 10000.0
EPS = 1e-6

V7X_LANES = 128
V7X_F32_SUBLANES = 8
V7X_BF16_SUBLANES = 16
V7X_VMEM_BYTES = 64 * 1024 * 1024

PROJ_ROW_TARGET = 1376
ROW_TILE_TARGET = 768
RET_ROW_TARGET = 3 * CHUNK
COL_TILE = 2048
HEAD_GROUP = 4
NORM_CHUNKS = 8
VMEM_LIMIT = V7X_VMEM_BYTES - 8 * 1024 * 1024


def _row_tile(m, target, quantum):
    best = quantum
    for t in range(quantum, target + 1, quantum):
        if m % t == 0:
            best = t
    return best


def _params(*sem):
    return pltpu.CompilerParams(dimension_semantics=sem, vmem_limit_bytes=VMEM_LIMIT)


def _rmsnorm(x, g):
    ms = jnp.mean(x * x, axis=-1, keepdims=True)
    return (x * lax.rsqrt(ms + EPS)) * g


def _embed_kernel(x_ref, meta_ref, g_ref, h_ref, u_ref, *, front):
    t = pl.program_id(1)

    @pl.when(t == 0)
    def _():
        d = h_ref.shape[1]
        rows = jnp.concatenate([jnp.zeros((front, d), F32), meta_ref[...]], axis=0)
        h_ref[...] = rows
        u_ref[...] = _rmsnorm(rows, g_ref[...]).astype(BF16)

    @pl.when(t > 0)
    def _():
        rows = x_ref[0]
        h_ref[...] = rows
        u_ref[...] = _rmsnorm(rows, g_ref[...]).astype(BF16)


def _embed(x, meta, g, *, front):
    batch, seq, d = x.shape
    nt = seq // CHUNK + 1
    m = batch * nt * CHUNK
    kern = functools.partial(_embed_kernel, front=front)
    return pl.pallas_call(
        kern,
        grid=(batch, nt),
        in_specs=[
            pl.BlockSpec((1, CHUNK, d), lambda b, t: (b, jnp.maximum(t - 1, 0), 0)),
            pl.BlockSpec((N_META, d), lambda b, t: (0, 0)),
            pl.BlockSpec((1, d), lambda b, t: (0, 0)),
        ],
        out_specs=[
            pl.BlockSpec((CHUNK, d), lambda b, t: (b * nt + t, 0)),
            pl.BlockSpec((CHUNK, d), lambda b, t: (b * nt + t, 0)),
        ],
        out_shape=[jax.ShapeDtypeStruct((m, d), F32), jax.ShapeDtypeStruct((m, d), BF16)],
        compiler_params=_params("parallel", "arbitrary"),
        name="embed",
    )(x, meta, g)


def _proj_main_kernel(u_ref, w_ref, cos_ref, sin_ref, gn_ref, o_ref, *,
                      tn, dk, qk_w, n_silu_lo, n_silu_hi, q_scale):
    j = pl.program_id(1)
    n_groups = tn // dk
    q_groups = qk_w // dk

    def matmul():
        return jnp.dot(u_ref[...], w_ref[...], preferred_element_type=F32)

    @pl.when(j == 0)
    def _():
        acc = matmul()
        cs = cos_ref[...]
        sn = sin_ref[...]
        for gidx in range(n_groups):
            cols = slice(gidx * dk, (gidx + 1) * dk)
            x = acc[:, cols]
            y = x * cs + pltpu.roll(x, dk // 2, 1) * sn
            if gidx < q_groups:
                y = y * q_scale
            o_ref[:, cols] = y.astype(BF16)

    @pl.when(j == 1)
    def _():
        o_ref[...] = matmul().astype(BF16)

    @pl.when(jnp.logical_and(j >= n_silu_lo, j < n_silu_hi))
    def _():
        o_ref[...] = (jax.nn.silu(matmul()) * gn_ref[...]).astype(BF16)

    @pl.when(j >= n_silu_hi)
    def _():
        o_ref[...] = jax.nn.sigmoid(matmul()).astype(BF16)


def _proj_main(u, w, cosf, sinf, gn_g, *, tm, dk, qk_w):
    m, d = u.shape
    n = w.shape[1]
    tn = COL_TILE
    v_w = 2 * qk_w
    assert 2 * qk_w == tn and v_w == tn and n % tn == 0
    n_silu_lo = (2 * qk_w + v_w) // tn
    n_silu_hi = n_silu_lo + v_w // tn
    tiles_per_seq = cosf.shape[0] // tm
    kern = functools.partial(_proj_main_kernel, tn=tn, dk=dk, qk_w=qk_w,
                             n_silu_lo=n_silu_lo, n_silu_hi=n_silu_hi, q_scale=dk ** -0.5)
    return pl.pallas_call(
        kern,
        grid=(m // tm, n // tn),
        in_specs=[
            pl.BlockSpec((tm, d), lambda i, j: (i, 0)),
            pl.BlockSpec((d, tn), lambda i, j: (0, j)),
            pl.BlockSpec((tm, dk), lambda i, j: (i % tiles_per_seq, 0)),
            pl.BlockSpec((tm, dk), lambda i, j: (i % tiles_per_seq, 0)),
            pl.BlockSpec((1, v_w), lambda i, j: (0, 0)),
        ],
        out_specs=pl.BlockSpec((tm, tn), lambda i, j: (i, j)),
        out_shape=jax.ShapeDtypeStruct((m, n), BF16),
        compiler_params=_params("parallel", "arbitrary"),
        name="proj_main",
    )(u, w, cosf, sinf, gn_g)


def _proj_conv_kernel(u_ref, w_ref, cw_ref, cb_ref, o_ref, p_ref, carry_ref, *, tm, tc):
    i = pl.program_id(0)
    j = pl.program_id(1)
    halo = V7X_F32_SUBLANES

    @pl.when(i == 0)
    def _():
        carry_ref[j] = jnp.zeros((halo, tc), F32)

    acc = jnp.dot(u_ref[...], w_ref[...], preferred_element_type=F32)

    p = acc[:, tc:2 * tc] * acc[:, 0:tc]
    p_ref[0:halo, :] = carry_ref[j]
    p_ref[halo:halo + tm, :] = p
    carry_ref[j] = p[tm - halo:tm, :]
    z = (cw_ref[0:1, :] * p_ref[halo - 2:halo - 2 + tm, :]
         + cw_ref[1:2, :] * p_ref[halo - 1:halo - 1 + tm, :]
         + cw_ref[2:3, :] * p) + cb_ref[...]
    gate = jax.nn.silu(acc[:, 3 * tc:4 * tc])
    o_ref[...] = ((acc[:, 2 * tc:3 * tc] * z) * gate).astype(BF16)


def _proj_conv(u, w, conv_w, conv_b, *, tm):
    m, d = u.shape
    n = w.shape[1]
    tn = COL_TILE
    tc = tn // 4
    nj = n // tn
    c_w = n // 4
    kern = functools.partial(_proj_conv_kernel, tm=tm, tc=tc)
    return pl.pallas_call(
        kern,
        grid=(m // tm, nj),
        in_specs=[
            pl.BlockSpec((tm, d), lambda i, j: (i, 0)),
            pl.BlockSpec((d, tn), lambda i, j: (0, j)),
            pl.BlockSpec((CONV_K, tc), lambda i, j: (0, j)),
            pl.BlockSpec((1, tc), lambda i, j: (0, j)),
        ],
        out_specs=pl.BlockSpec((tm, tc), lambda i, j: (i, j)),
        out_shape=jax.ShapeDtypeStruct((m, c_w), BF16),
        scratch_shapes=[pltpu.VMEM((tm + V7X_F32_SUBLANES, tc), F32),
                        pltpu.VMEM((nj, V7X_F32_SUBLANES, tc), F32)],
        compiler_params=_params("arbitrary", "arbitrary"),
        name="proj_conv",
    )(u, w, conv_w, conv_b)


def _retention_kernel(cd_ref, q_ref, k_ref, v_ref, sg_ref, dec_ref, qd_ref, kd_ref,
                      o_ref, s_ref, *, tt, heads, dk, dv):
    @pl.when(pl.program_id(1) == 0)
    def _():
        s_ref[...] = jnp.zeros_like(s_ref)

    nt = (((1,), (1,)), ((), ()))
    tn = (((0,), (0,)), ((), ()))

    kcs = [slice(hd * dk, (hd + 1) * dk) for hd in range(heads)]
    vcs = [slice(hd * dv, (hd + 1) * dv) for hd in range(heads)]
    for c in range(tt // CHUNK):
        r = slice(c * CHUNK, (c + 1) * CHUNK)
        for g0 in range(0, heads, HEAD_GROUP):
            hs = range(g0, g0 + HEAD_GROUP)
            scores = {hd: lax.dot_general(q_ref[r, kcs[hd]], k_ref[r, kcs[hd]], nt,
                                          preferred_element_type=F32) for hd in hs}
            ys = {}
            for hd in hs:
                lhs = jnp.concatenate(
                    [(scores[hd] * dec_ref[hd]).astype(BF16),
                     (q_ref[r, kcs[hd]].astype(F32) * qd_ref[hd]).astype(BF16)], axis=1)
                rhs = jnp.concatenate([v_ref[r, vcs[hd]], s_ref[hd].astype(BF16)], axis=0)
                ys[hd] = jnp.dot(lhs, rhs, preferred_element_type=F32)
            for hd in hs:
                kdec = (k_ref[r, kcs[hd]].astype(F32) * kd_ref[hd]).astype(BF16)
                s_ref[hd] = s_ref[hd] * cd_ref[hd] + lax.dot_general(
                    kdec, v_ref[r, vcs[hd]], tn, preferred_element_type=F32)
            for hd in hs:
                y = ys[hd]
                mu = jnp.mean(y, axis=-1, keepdims=True)
                yc = y - mu
                var = jnp.mean(yc * yc, axis=-1, keepdims=True)
                yn = yc * lax.rsqrt(var + EPS)
                o_ref[r, vcs[hd]] = (yn * sg_ref[r, vcs[hd]].astype(F32)).astype(BF16)


def _retention(p, chunk_decay, decay, q_decay, k_decay, *, batch, lp, heads, dk, dv, tt):
    m = p.shape[0]
    qk_w = heads * dk
    v_w = heads * dv
    nt = lp // tt
    kern = functools.partial(_retention_kernel, tt=tt, heads=heads, dk=dk, dv=dv)
    row = lambda b, t: b * nt + t
    return pl.pallas_call(
        kern,
        grid=(batch, nt),
        in_specs=[
            pl.BlockSpec(memory_space=pltpu.SMEM),
            pl.BlockSpec((tt, qk_w), lambda b, t: (row(b, t), 0)),
            pl.BlockSpec((tt, qk_w), lambda b, t: (row(b, t), 1)),
            pl.BlockSpec((tt, v_w), lambda b, t: (row(b, t), 1)),
            pl.BlockSpec((tt, v_w), lambda b, t: (row(b, t), 2)),
            pl.BlockSpec((heads, CHUNK, CHUNK), lambda b, t: (0, 0, 0)),
            pl.BlockSpec((heads, CHUNK, dk), lambda b, t: (0, 0, 0)),
            pl.BlockSpec((heads, CHUNK, dk), lambda b, t: (0, 0, 0)),
        ],
        out_specs=pl.BlockSpec((tt, v_w), lambda b, t: (row(b, t), 0)),
        out_shape=jax.ShapeDtypeStruct((m, v_w), BF16),
        scratch_shapes=[pltpu.VMEM((heads, dk, dv), F32)],
        compiler_params=_params("arbitrary", "arbitrary"),
        name="retention",
    )(chunk_decay, p, p, p, p, decay, q_decay, k_decay)


def _branch_kernel(yr_ref, yc_ref, w0_ref, w1_ref, mr_ref, mc_ref, o_ref):
    a = jnp.dot(yr_ref[...], w0_ref[...], preferred_element_type=F32)
    b = jnp.dot(yc_ref[...], w1_ref[...], preferred_element_type=F32)
    o_ref[...] = (mr_ref[...].astype(F32) * a + mc_ref[...].astype(F32) * b).astype(BF16)


def _branch(yr, yc, w0, w1, p, *, tm, gate_col0):
    m, kdim = yr.shape
    n = w0.shape[1]
    tn = COL_TILE
    nj = n // tn
    g0 = gate_col0 // tn
    return pl.pallas_call(
        _branch_kernel,
        grid=(m // tm, nj),
        in_specs=[
            pl.BlockSpec((tm, kdim), lambda i, j: (i, 0)),
            pl.BlockSpec((tm, kdim), lambda i, j: (i, 0)),
            pl.BlockSpec((kdim, tn), lambda i, j: (0, j)),
            pl.BlockSpec((kdim, tn), lambda i, j: (0, j)),
            pl.BlockSpec((tm, tn), lambda i, j: (i, g0 + j)),
            pl.BlockSpec((tm, tn), lambda i, j: (i, g0 + nj + j)),
        ],
        out_specs=pl.BlockSpec((tm, tn), lambda i, j: (i, j)),
        out_shape=jax.ShapeDtypeStruct((m, n), BF16),
        compiler_params=_params("parallel", "arbitrary"),
        name="branch",
    )(yr, yc, w0, w1, p, p)


def _out_kernel(a_ref, w_ref, h_ref, o_ref):
    o_ref[...] = h_ref[...] + jnp.dot(a_ref[...], w_ref[...], preferred_element_type=F32)


def _out_norm_kernel(a_ref, w_ref, h_ref, g_ref, o_ref, u_ref):
    hn = h_ref[...] + jnp.dot(a_ref[...], w_ref[...], preferred_element_type=F32)
    o_ref[...] = hn
    u_ref[...] = _rmsnorm(hn, g_ref[...]).astype(BF16)


def _out_proj(a, w, h, g_next, *, tm):
    m, kdim = a.shape
    n = w.shape[1]
    row_spec = lambda width: pl.BlockSpec((tm, width), lambda i: (i, 0))
    in_specs = [row_spec(kdim),
                pl.BlockSpec((kdim, n), lambda i: (0, 0), pipeline_mode=pl.Buffered(1)),
                row_spec(n)]
    if g_next is None:
        return pl.pallas_call(
            _out_kernel, grid=(m // tm,), in_specs=in_specs, out_specs=row_spec(n),
            out_shape=jax.ShapeDtypeStruct((m, n), F32),
            compiler_params=_params("parallel"), name="out_proj",
        )(a, w, h), None
    return pl.pallas_call(
        _out_norm_kernel, grid=(m // tm,),
        in_specs=in_specs + [pl.BlockSpec((1, n), lambda i: (0, 0))],
        out_specs=[row_spec(n), row_spec(n)],
        out_shape=[jax.ShapeDtypeStruct((m, n), F32), jax.ShapeDtypeStruct((m, n), BF16)],
        compiler_params=_params("parallel"), name="out_proj_norm",
    )(a, w, h, g_next)


def _final_norm_kernel(a_ref, b_ref, g_ref, o_ref, *, nc):
    g = g_ref[...]
    for c in range(nc - 1):
        o_ref[0, c] = _rmsnorm(a_ref[0, c + 1], g)
    o_ref[0, nc - 1] = _rmsnorm(b_ref[0, 0], g)


def _final_norm(h4, g, *, nc):
    b, n_chunks, _, d = h4.shape
    kern = functools.partial(_final_norm_kernel, nc=nc)
    return pl.pallas_call(
        kern,
        grid=(b, (n_chunks - 1) // nc),
        in_specs=[
            pl.BlockSpec((1, nc, CHUNK, d), lambda bi, t: (bi, t, 0, 0)),
            pl.BlockSpec((1, 1, CHUNK, d), lambda bi, t: (bi, nc * t + nc, 0, 0)),
            pl.BlockSpec((1, d), lambda bi, t: (0, 0)),
        ],
        out_specs=pl.BlockSpec((1, nc, CHUNK, d), lambda bi, t: (bi, t, 0, 0)),
        out_shape=jax.ShapeDtypeStruct((b, n_chunks - 1, CHUNK, d), F32),
        compiler_params=_params("parallel", "parallel"),
        name="final_norm",
    )(h4, h4, g)


def _tables(lp, front, dk, heads):
    f32 = np.float32
    half = dk // 2
    inv_freq = f32(ROPE_BASE) ** (-np.arange(half, dtype=f32) / f32(half))
    pos = np.maximum(np.arange(lp, dtype=np.int32) - front, 0).astype(f32)
    ang = pos[:, None] * inv_freq[None, :]
    cos = np.cos(ang).astype(f32)
    sin = np.sin(ang).astype(f32)
    cosf = np.concatenate([cos, cos], axis=-1)
    sinf = np.concatenate([-sin, sin], axis=-1)

    log_g = np.log(f32(1.0) - np.exp2(f32(-5.0) - np.arange(heads, dtype=f32))).astype(f32)
    idx = np.arange(CHUNK, dtype=f32)
    diff = idx[:, None] - idx[None, :]
    decay = np.where(diff >= 0, np.exp(log_g[:, None, None] * np.maximum(diff, f32(0.0))),
                     f32(0.0)).astype(f32)
    q_decay = np.broadcast_to(np.exp(log_g[:, None] * (idx + f32(1.0)))[:, :, None],
                              (heads, CHUNK, dk)).astype(f32)
    k_decay = np.broadcast_to(np.exp(log_g[:, None] * (f32(CHUNK - 1.0) - idx))[:, :, None],
                              (heads, CHUNK, dk)).astype(f32)
    chunk_decay = np.exp(log_g * f32(CHUNK)).astype(f32)
    return cosf, sinf, decay, q_decay, k_decay, chunk_decay


def kernel(x, meta_tokens, norm_g, w_in, conv_w, conv_b, gn_g, w_branch, w_out, final_norm_g):
    batch, seq, d = x.shape
    depth = w_in.shape[0]
    heads = RET_HEADS
    dk = d // 16
    dv = 2 * dk
    qk_w = heads * dk
    v_w = heads * dv
    c_w = d
    assert dk == V7X_LANES and seq % (CHUNK * NORM_CHUNKS) == 0

    front = CHUNK - N_META
    lp = front + N_META + seq
    m = batch * lp
    tm_proj = _row_tile(lp, PROJ_ROW_TARGET, V7X_BF16_SUBLANES)
    tm = _row_tile(m, ROW_TILE_TARGET, CHUNK)
    tt = _row_tile(lp, RET_ROW_TARGET, CHUNK)

    cosf, sinf, decay, q_decay, k_decay, chunk_decay = _tables(lp, front, dk, heads)

    o_cx = 2 * qk_w + 2 * v_w
    o_mr = o_cx + 4 * c_w
    tc = COL_TILE // 4

    h, u = _embed(x, meta_tokens.astype(x.dtype), norm_g[0][None, :], front=front)

    for layer in range(depth):
        w = w_in[layer]
        w_main = jnp.concatenate([w[:, :o_cx], w[:, o_mr:]], axis=1).astype(BF16)
        w_conv = (w[:, o_cx:o_mr].reshape(d, 4, c_w // tc, tc).transpose(0, 2, 1, 3)
                  .reshape(d, 4 * c_w).astype(BF16))

        p = _proj_main(u, w_main, cosf, sinf, gn_g[layer][None, :], tm=tm_proj, dk=dk, qk_w=qk_w)
        yc = _proj_conv(u, w_conv, conv_w[layer], conv_b[layer][None, :], tm=tm_proj)
        yr = _retention(p, chunk_decay, decay, q_decay, k_decay,
                        batch=batch, lp=lp, heads=heads, dk=dk, dv=dv, tt=tt)
        merged = _branch(yr, yc, w_branch[layer, 0].astype(BF16), w_branch[layer, 1].astype(BF16),
                         p, tm=tm, gate_col0=2 * qk_w + 2 * v_w)
        g_next = norm_g[layer + 1][None, :] if layer + 1 < depth else None
        h, u = _out_proj(merged, w_out[layer].astype(BF16), h, g_next, tm=tm)

    out = _final_norm(h.reshape(batch, lp // CHUNK, CHUNK, d), final_norm_g[None, :],
                      nc=NORM_CHUNKS)
    return out.reshape(batch, seq, d)
```

```python
import functools

import numpy as np
import jax
import jax.numpy as jnp
from jax import lax
from jax.experimental import pallas as pl
from jax.experimental.pallas import tpu as pltpu

F32 = jnp.float32
BF16 = jnp.bfloat16

N_META = 16
RET_HEADS = 8
CONV_K = 3
CHUNK = 128
ROPE_BASE = 10000.0
EPS = 1e-6

V7X_LANES = 128
V7X_F32_SUBLANES = 8
V7X_BF16_SUBLANES = 16
V7X_VMEM_BYTES = 64 * 1024 * 1024

PROJ_ROW_TARGET = 1376
ROW_TILE_TARGET = 768
RET_ROW_TARGET = 3 * CHUNK
COL_TILE = 2048
HEAD_GROUP = 4
NORM_CHUNKS = 8
VMEM_LIMIT = V7X_VMEM_BYTES - 8 * 1024 * 1024


def _row_tile(m, target, quantum):
    best = quantum
    for t in range(quantum, target + 1, quantum):
        if m % t == 0:
            best = t
    return best


def _params(*sem):
    return pltpu.CompilerParams(dimension_semantics=sem, vmem_limit_bytes=VMEM_LIMIT)


def _rmsnorm(x, g):
    ms = jnp.mean(x * x, axis=-1, keepdims=True)
    return (x * lax.rsqrt(ms + EPS)) * g


def _embed_kernel(x_ref, meta_ref, g_ref, h_ref, u_ref, *, front):
    t = pl.program_id(1)

    @pl.when(t == 0)
    def _():
        d = h_ref.shape[1]
        rows = jnp.concatenate([jnp.zeros((front, d), F32), meta_ref[...]], axis=0)
        h_ref[...] = rows
        u_ref[...] = _rmsnorm(rows, g_ref[...]).astype(BF16)

    @pl.when(t > 0)
    def _():
        rows = x_ref[0]
        h_ref[...] = rows
        u_ref[...] = _rmsnorm(rows, g_ref[...]).astype(BF16)


def _embed(x, meta, g, *, front):
    batch, seq, d = x.shape
    nt = seq // CHUNK + 1
    m = batch * nt * CHUNK
    kern = functools.partial(_embed_kernel, front=front)
    return pl.pallas_call(
        kern,
        grid=(batch, nt),
        in_specs=[
            pl.BlockSpec((1, CHUNK, d), lambda b, t: (b, jnp.maximum(t - 1, 0), 0)),
            pl.BlockSpec((N_META, d), lambda b, t: (0, 0)),
            pl.BlockSpec((1, d), lambda b, t: (0, 0)),
        ],
        out_specs=[
            pl.BlockSpec((CHUNK, d), lambda b, t: (b * nt + t, 0)),
            pl.BlockSpec((CHUNK, d), lambda b, t: (b * nt + t, 0)),
        ],
        out_shape=[jax.ShapeDtypeStruct((m, d), F32), jax.ShapeDtypeStruct((m, d), BF16)],
        compiler_params=_params("parallel", "arbitrary"),
        name="embed",
    )(x, meta, g)


def _proj_kernel(*refs, kind, dk, q_groups, q_scale):
    u_ref, w_ref, o_ref = refs[0], refs[1], refs[-1]
    acc = jnp.dot(u_ref[...], w_ref[...], preferred_element_type=F32)
    if kind == "rotary":
        cs = refs[2][...]
        sn = refs[3][...]
        for gidx in range(acc.shape[1] // dk):
            cols = slice(gidx * dk, (gidx + 1) * dk)
            x = acc[:, cols]
            y = x * cs + pltpu.roll(x, dk // 2, 1) * sn
            if gidx < q_groups:
                y = y * q_scale
            o_ref[:, cols] = y.astype(BF16)
    elif kind == "plain":
        o_ref[...] = acc.astype(BF16)
    elif kind == "gain_silu":
        o_ref[...] = (jax.nn.silu(acc) * refs[2][...]).astype(BF16)
    else:
        assert kind == "sigmoid"
        o_ref[...] = jax.nn.sigmoid(acc).astype(BF16)


def _proj(u, w_all, layer, col0, n_cols, kind, extras=(), *, tm, dk=0, q_groups=0):
    m, d = u.shape
    tn = COL_TILE
    assert col0 % tn == 0 and n_cols % tn == 0
    c0 = col0 // tn
    extra_specs = []
    for e in extras:
        if e.shape[0] == 1:
            extra_specs.append(pl.BlockSpec((1, tn), lambda i, j: (0, j)))
        else:
            tiles_per_seq = e.shape[0] // tm
            extra_specs.append(pl.BlockSpec((tm, e.shape[1]),
                                            lambda i, j, t=tiles_per_seq: (i % t, 0)))
    kern = functools.partial(_proj_kernel, kind=kind, dk=dk, q_groups=q_groups,
                             q_scale=dk ** -0.5 if dk else 1.0)
    return pl.pallas_call(
        kern,
        grid=(m // tm, n_cols // tn),
        in_specs=[
            pl.BlockSpec((tm, d), lambda i, j: (i, 0)),
            pl.BlockSpec((None, d, tn), lambda i, j: (layer, 0, c0 + j)),
        ] + extra_specs,
        out_specs=pl.BlockSpec((tm, tn), lambda i, j: (i, j)),
        out_shape=jax.ShapeDtypeStruct((m, n_cols), BF16),
        compiler_params=_params("parallel", "arbitrary"),
        name="proj_" + kind,
    )(u, w_all, *extras)


def _proj_conv_kernel(u_ref, wx_ref, wpre_ref, wpost_ref, wg_ref, cw_ref, cb_ref, o_ref,
                      p_ref, carry_ref, *, tm, tc):
    i = pl.program_id(0)
    j = pl.program_id(1)
    halo = V7X_F32_SUBLANES

    @pl.when(i == 0)
    def _():
        carry_ref[j] = jnp.zeros((halo, tc), F32)

    def proj(w_ref):
        return jnp.dot(u_ref[...], w_ref[...], preferred_element_type=F32)

    p = proj(wpre_ref) * proj(wx_ref)
    p_ref[0:halo, :] = carry_ref[j]
    p_ref[halo:halo + tm, :] = p
    carry_ref[j] = p[tm - halo:tm, :]
    z = (cw_ref[0:1, :] * p_ref[halo - 2:halo - 2 + tm, :]
         + cw_ref[1:2, :] * p_ref[halo - 1:halo - 1 + tm, :]
         + cw_ref[2:3, :] * p) + cb_ref[...]
    gate = jax.nn.silu(proj(wg_ref))
    o_ref[...] = ((proj(wpost_ref) * z) * gate).astype(BF16)


def _proj_conv(u, w_all, layer, col0, c_w, conv_w, conv_b, *, tm):
    m, d = u.shape
    tc = COL_TILE // 4
    nj = c_w // tc
    assert col0 % tc == 0 and c_w % tc == 0

    def w_spec(section):
        c0 = (col0 + section * c_w) // tc
        return pl.BlockSpec((None, d, tc), lambda i, j: (layer, 0, c0 + j))

    kern = functools.partial(_proj_conv_kernel, tm=tm, tc=tc)
    return pl.pallas_call(
        kern,
        grid=(m // tm, nj),
        in_specs=[
            pl.BlockSpec((tm, d), lambda i, j: (i, 0)),
            w_spec(0), w_spec(1), w_spec(2), w_spec(3),
            pl.BlockSpec((CONV_K, tc), lambda i, j: (0, j)),
            pl.BlockSpec((1, tc), lambda i, j: (0, j)),
        ],
        out_specs=pl.BlockSpec((tm, tc), lambda i, j: (i, j)),
        out_shape=jax.ShapeDtypeStruct((m, c_w), BF16),
        scratch_shapes=[pltpu.VMEM((tm + V7X_F32_SUBLANES, tc), F32),
                        pltpu.VMEM((nj, V7X_F32_SUBLANES, tc), F32)],
        compiler_params=_params("arbitrary", "arbitrary"),
        name="proj_conv",
    )(u, w_all, w_all, w_all, w_all, conv_w, conv_b)


def _retention_kernel(cd_ref, q_ref, k_ref, v_ref, sg_ref, dec_ref, qd_ref, kd_ref,
                      o_ref, s_ref, *, tt, heads, dk, dv):
    @pl.when(pl.program_id(1) == 0)
    def _():
        s_ref[...] = jnp.zeros_like(s_ref)

    nt = (((1,), (1,)), ((), ()))
    tn = (((0,), (0,)), ((), ()))

    kcs = [slice(hd * dk, (hd + 1) * dk) for hd in range(heads)]
    vcs = [slice(hd * dv, (hd + 1) * dv) for hd in range(heads)]
    for c in range(tt // CHUNK):
        r = slice(c * CHUNK, (c + 1) * CHUNK)
        for g0 in range(0, heads, HEAD_GROUP):
            hs = range(g0, g0 + HEAD_GROUP)
            scores = {hd: lax.dot_general(q_ref[r, kcs[hd]], k_ref[r, kcs[hd]], nt,
                                          preferred_element_type=F32) for hd in hs}
            ys = {}
            for hd in hs:
                lhs = jnp.concatenate(
                    [(scores[hd] * dec_ref[hd]).astype(BF16),
                     (q_ref[r, kcs[hd]].astype(F32) * qd_ref[hd]).astype(BF16)], axis=1)
                rhs = jnp.concatenate([v_ref[r, vcs[hd]], s_ref[hd].astype(BF16)], axis=0)
                ys[hd] = jnp.dot(lhs, rhs, preferred_element_type=F32)
            for hd in hs:
                kdec = (k_ref[r, kcs[hd]].astype(F32) * kd_ref[hd]).astype(BF16)
                s_ref[hd] = s_ref[hd] * cd_ref[hd] + lax.dot_general(
                    kdec, v_ref[r, vcs[hd]], tn, preferred_element_type=F32)
            for hd in hs:
                y = ys[hd]
                mu = jnp.mean(y, axis=-1, keepdims=True)
                yc = y - mu
                var = jnp.mean(yc * yc, axis=-1, keepdims=True)
                yn = yc * lax.rsqrt(var + EPS)
                o_ref[r, vcs[hd]] = (yn * sg_ref[r, vcs[hd]].astype(F32)).astype(BF16)


def _retention(qk, v, sg, chunk_decay, decay, q_decay, k_decay, *, batch, lp, heads, dk, dv, tt):
    m = qk.shape[0]
    qk_w = heads * dk
    v_w = heads * dv
    nt = lp // tt
    kern = functools.partial(_retention_kernel, tt=tt, heads=heads, dk=dk, dv=dv)
    row = lambda b, t: b * nt + t
    return pl.pallas_call(
        kern,
        grid=(batch, nt),
        in_specs=[
            pl.BlockSpec(memory_space=pltpu.SMEM),
            pl.BlockSpec((tt, qk_w), lambda b, t: (row(b, t), 0)),
            pl.BlockSpec((tt, qk_w), lambda b, t: (row(b, t), 1)),
            pl.BlockSpec((tt, v_w), lambda b, t: (row(b, t), 0)),
            pl.BlockSpec((tt, v_w), lambda b, t: (row(b, t), 0)),
            pl.BlockSpec((heads, CHUNK, CHUNK), lambda b, t: (0, 0, 0)),
            pl.BlockSpec((heads, CHUNK, dk), lambda b, t: (0, 0, 0)),
            pl.BlockSpec((heads, CHUNK, dk), lambda b, t: (0, 0, 0)),
        ],
        out_specs=pl.BlockSpec((tt, v_w), lambda b, t: (row(b, t), 0)),
        out_shape=jax.ShapeDtypeStruct((m, v_w), BF16),
        scratch_shapes=[pltpu.VMEM((heads, dk, dv), F32)],
        compiler_params=_params("arbitrary", "arbitrary"),
        name="retention",
    )(chunk_decay, qk, qk, v, sg, decay, q_decay, k_decay)


def _branch_kernel(yr_ref, yc_ref, w0_ref, w1_ref, mr_ref, mc_ref, o_ref):
    a = jnp.dot(yr_ref[...], w0_ref[...], preferred_element_type=F32)
    b = jnp.dot(yc_ref[...], w1_ref[...], preferred_element_type=F32)
    o_ref[...] = (mr_ref[...].astype(F32) * a + mc_ref[...].astype(F32) * b).astype(BF16)


def _branch(yr, yc, w_all, layer, gates, *, tm):
    m, kdim = yr.shape
    n = w_all.shape[-1]
    tn = COL_TILE
    nj = n // tn
    return pl.pallas_call(
        _branch_kernel,
        grid=(m // tm, nj),
        in_specs=[
            pl.BlockSpec((tm, kdim), lambda i, j: (i, 0)),
            pl.BlockSpec((tm, kdim), lambda i, j: (i, 0)),
            pl.BlockSpec((None, None, kdim, tn), lambda i, j: (layer, 0, 0, j)),
            pl.BlockSpec((None, None, kdim, tn), lambda i, j: (layer, 1, 0, j)),
            pl.BlockSpec((tm, tn), lambda i, j: (i, j)),
            pl.BlockSpec((tm, tn), lambda i, j: (i, nj + j)),
        ],
        out_specs=pl.BlockSpec((tm, tn), lambda i, j: (i, j)),
        out_shape=jax.ShapeDtypeStruct((m, n), BF16),
        compiler_params=_params("parallel", "arbitrary"),
        name="branch",
    )(yr, yc, w_all, w_all, gates, gates)


def _out_kernel(a_ref, w_ref, h_ref, o_ref):
    o_ref[...] = h_ref[...] + jnp.dot(a_ref[...], w_ref[...], preferred_element_type=F32)


def _out_norm_kernel(a_ref, w_ref, h_ref, g_ref, o_ref, u_ref):
    hn = h_ref[...] + jnp.dot(a_ref[...], w_ref[...], preferred_element_type=F32)
    o_ref[...] = hn
    u_ref[...] = _rmsnorm(hn, g_ref[...]).astype(BF16)


def _out_proj(a, w_all, layer, h, g_next, *, tm):
    m, kdim = a.shape
    n = w_all.shape[-1]
    row_spec = lambda width: pl.BlockSpec((tm, width), lambda i: (i, 0))
    in_specs = [row_spec(kdim),
                pl.BlockSpec((None, kdim, n), lambda i: (layer, 0, 0),
                             pipeline_mode=pl.Buffered(1)),
                row_spec(n)]
    if g_next is None:
        return pl.pallas_call(
            _out_kernel, grid=(m // tm,), in_specs=in_specs, out_specs=row_spec(n),
            out_shape=jax.ShapeDtypeStruct((m, n), F32),
            compiler_params=_params("parallel"), name="out_proj",
        )(a, w_all, h), None
    return pl.pallas_call(
        _out_norm_kernel, grid=(m // tm,),
        in_specs=in_specs + [pl.BlockSpec((1, n), lambda i: (0, 0))],
        out_specs=[row_spec(n), row_spec(n)],
        out_shape=[jax.ShapeDtypeStruct((m, n), F32), jax.ShapeDtypeStruct((m, n), BF16)],
        compiler_params=_params("parallel"), name="out_proj_norm",
    )(a, w_all, h, g_next)


def _final_norm_kernel(a_ref, b_ref, g_ref, o_ref, *, nc):
    g = g_ref[...]
    for c in range(nc - 1):
        o_ref[0, c] = _rmsnorm(a_ref[0, c + 1], g)
    o_ref[0, nc - 1] = _rmsnorm(b_ref[0, 0], g)


def _final_norm(h4, g, *, nc):
    b, n_chunks, _, d = h4.shape
    kern = functools.partial(_final_norm_kernel, nc=nc)
    return pl.pallas_call(
        kern,
        grid=(b, (n_chunks - 1) // nc),
        in_specs=[
            pl.BlockSpec((1, nc, CHUNK, d), lambda bi, t: (bi, t, 0, 0)),
            pl.BlockSpec((1, 1, CHUNK, d), lambda bi, t: (bi, nc * t + nc, 0, 0)),
            pl.BlockSpec((1, d), lambda bi, t: (0, 0)),
        ],
        out_specs=pl.BlockSpec((1, nc, CHUNK, d), lambda bi, t: (bi, t, 0, 0)),
        out_shape=jax.ShapeDtypeStruct((b, n_chunks - 1, CHUNK, d), F32),
        compiler_params=_params("parallel", "parallel"),
        name="final_norm",
    )(h4, h4, g)


def _tables(lp, front, dk, heads):
    f32 = np.float32
    half = dk // 2
    inv_freq = f32(ROPE_BASE) ** (-np.arange(half, dtype=f32) / f32(half))
    pos = np.maximum(np.arange(lp, dtype=np.int32) - front, 0).astype(f32)
    ang = pos[:, None] * inv_freq[None, :]
    cos = np.cos(ang).astype(f32)
    sin = np.sin(ang).astype(f32)
    cosf = np.concatenate([cos, cos], axis=-1)
    sinf = np.concatenate([-sin, sin], axis=-1)

    log_g = np.log(f32(1.0) - np.exp2(f32(-5.0) - np.arange(heads, dtype=f32))).astype(f32)
    idx = np.arange(CHUNK, dtype=f32)
    diff = idx[:, None] - idx[None, :]
    decay = np.where(diff >= 0, np.exp(log_g[:, None, None] * np.maximum(diff, f32(0.0))),
                     f32(0.0)).astype(f32)
    q_decay = np.broadcast_to(np.exp(log_g[:, None] * (idx + f32(1.0)))[:, :, None],
                              (heads, CHUNK, dk)).astype(f32)
    k_decay = np.broadcast_to(np.exp(log_g[:, None] * (f32(CHUNK - 1.0) - idx))[:, :, None],
                              (heads, CHUNK, dk)).astype(f32)
    chunk_decay = np.exp(log_g * f32(CHUNK)).astype(f32)
    return cosf, sinf, decay, q_decay, k_decay, chunk_decay


def kernel(x, meta_tokens, norm_g, w_in, conv_w, conv_b, gn_g, w_branch, w_out, final_norm_g):
    batch, seq, d = x.shape
    depth = w_in.shape[0]
    heads = RET_HEADS
    dk = d // 16
    dv = 2 * dk
    qk_w = heads * dk
    v_w = heads * dv
    c_w = d
    assert dk == V7X_LANES and seq % (CHUNK * NORM_CHUNKS) == 0

    front = CHUNK - N_META
    lp = front + N_META + seq
    m = batch * lp
    tm_proj = _row_tile(lp, PROJ_ROW_TARGET, V7X_BF16_SUBLANES)
    tm = _row_tile(m, ROW_TILE_TARGET, CHUNK)
    tt = _row_tile(lp, RET_ROW_TARGET, CHUNK)

    cosf, sinf, decay, q_decay, k_decay, chunk_decay = _tables(lp, front, dk, heads)

    o_v = 2 * qk_w
    o_gr = o_v + v_w
    o_cx = o_gr + v_w
    o_mr = o_cx + 4 * c_w

    w_in_b = w_in.astype(BF16)
    w_branch_b = w_branch.astype(BF16)
    w_out_b = w_out.astype(BF16)

    h, u = _embed(x, meta_tokens.astype(x.dtype), norm_g[0][None, :], front=front)

    for layer in range(depth):
        proj = functools.partial(_proj, u, w_in_b, layer, tm=tm_proj)
        qk = proj(0, 2 * qk_w, "rotary", (cosf, sinf), dk=dk, q_groups=heads)
        v = proj(o_v, v_w, "plain")
        sg = proj(o_gr, v_w, "gain_silu", (gn_g[layer][None, :],))
        gates = proj(o_mr, 2 * d, "sigmoid")
        yc = _proj_conv(u, w_in_b, layer, o_cx, c_w, conv_w[layer], conv_b[layer][None, :],
                        tm=tm_proj)
        yr = _retention(qk, v, sg, chunk_decay, decay, q_decay, k_decay,
                        batch=batch, lp=lp, heads=heads, dk=dk, dv=dv, tt=tt)
        merged = _branch(yr, yc, w_branch_b, layer, gates, tm=tm)
        g_next = norm_g[layer + 1][None, :] if layer + 1 < depth else None
        h, u = _out_proj(merged, w_out_b, layer, h, g_next, tm=tm)

    out = _final_norm(h.reshape(batch, lp // CHUNK, CHUNK, d), final_norm_g[None, :],
                      nc=NORM_CHUNKS)
    return out.reshape(batch, seq, d)
```

```python
import functools

import numpy as np
import jax
import jax.numpy as jnp
from jax import lax
from jax.experimental import pallas as pl
from jax.experimental.pallas import tpu as pltpu

F32 = jnp.float32
BF16 = jnp.bfloat16

N_META = 16
RET_HEADS = 8
CONV_K = 3
CHUNK = 128
ROPE_BASE = 10000.0
EPS = 1e-6

V7X_LANES = 128
V7X_F32_SUBLANES = 8
V7X_BF16_SUBLANES = 16
V7X_VMEM_BYTES = 64 * 1024 * 1024

PROJ_ROW_TARGET = 1376
ROW_TILE_TARGET = 768
COL_TILE = 2048
GATE_PIECE = 256
NORM_CHUNKS = 8
VMEM_LIMIT = V7X_VMEM_BYTES - 8 * 1024 * 1024


def _row_tile(m, target, quantum):
    best = quantum
    for t in range(quantum, target + 1, quantum):
        if m % t == 0:
            best = t
    return best


def _params(*sem):
    return pltpu.CompilerParams(dimension_semantics=sem, vmem_limit_bytes=VMEM_LIMIT)


def _rmsnorm(x, g):
    ms = jnp.mean(x * x, axis=-1, keepdims=True)
    return (x * lax.rsqrt(ms + EPS)) * g


def _embed_kernel(x_ref, meta_ref, g_ref, h_ref, u_ref, *, front):
    t = pl.program_id(1)

    @pl.when(t == 0)
    def _():
        d = h_ref.shape[1]
        rows = jnp.concatenate([jnp.zeros((front, d), F32), meta_ref[...]], axis=0)
        h_ref[...] = rows
        u_ref[...] = _rmsnorm(rows, g_ref[...]).astype(BF16)

    @pl.when(t > 0)
    def _():
        rows = x_ref[0]
        h_ref[...] = rows
        u_ref[...] = _rmsnorm(rows, g_ref[...]).astype(BF16)


def _embed(x, meta, g, *, front):
    batch, seq, d = x.shape
    nt = seq // CHUNK + 1
    m = batch * nt * CHUNK
    kern = functools.partial(_embed_kernel, front=front)
    return pl.pallas_call(
        kern,
        grid=(batch, nt),
        in_specs=[
            pl.BlockSpec((1, CHUNK, d), lambda b, t: (b, jnp.maximum(t - 1, 0), 0)),
            pl.BlockSpec((N_META, d), lambda b, t: (0, 0)),
            pl.BlockSpec((1, d), lambda b, t: (0, 0)),
        ],
        out_specs=[
            pl.BlockSpec((CHUNK, d), lambda b, t: (b * nt + t, 0)),
            pl.BlockSpec((CHUNK, d), lambda b, t: (b * nt + t, 0)),
        ],
        out_shape=[jax.ShapeDtypeStruct((m, d), F32), jax.ShapeDtypeStruct((m, d), BF16)],
        compiler_params=_params("parallel", "arbitrary"),
        name="embed",
    )(x, meta, g)


def _proj_kernel(*refs, kind, dk, q_groups, q_scale):
    u_ref, w_ref, o_ref = refs[0], refs[1], refs[-1]
    acc = jnp.dot(u_ref[...], w_ref[...], preferred_element_type=F32)
    if kind == "rotary":
        cs = refs[2][...]
        sn = refs[3][...]
        for gidx in range(acc.shape[1] // dk):
            cols = slice(gidx * dk, (gidx + 1) * dk)
            x = acc[:, cols]
            y = x * cs + pltpu.roll(x, dk // 2, 1) * sn
            if gidx < q_groups:
                y = y * q_scale
            o_ref[:, cols] = y.astype(BF16)
    elif kind == "plain":
        o_ref[...] = acc.astype(BF16)
    else:
        assert kind == "gain_silu"
        o_ref[...] = (jax.nn.silu(acc) * refs[2][...]).astype(BF16)


def _proj(u, w_all, layer, col0, n_cols, kind, extras=(), *, tm, dk=0, q_groups=0):
    m, d = u.shape
    tn = COL_TILE
    assert col0 % tn == 0 and n_cols % tn == 0
    c0 = col0 // tn
    extra_specs = []
    for e in extras:
        if e.shape[0] == 1:
            extra_specs.append(pl.BlockSpec((1, tn), lambda i, j: (0, j)))
        else:
            tiles_per_seq = e.shape[0] // tm
            extra_specs.append(pl.BlockSpec((tm, e.shape[1]),
                                            lambda i, j, t=tiles_per_seq: (i % t, 0)))
    kern = functools.partial(_proj_kernel, kind=kind, dk=dk, q_groups=q_groups,
                             q_scale=dk ** -0.5 if dk else 1.0)
    return pl.pallas_call(
        kern,
        grid=(m // tm, n_cols // tn),
        in_specs=[
            pl.BlockSpec((tm, d), lambda i, j: (i, 0)),
            pl.BlockSpec((None, d, tn), lambda i, j: (layer, 0, c0 + j)),
        ] + extra_specs,
        out_specs=pl.BlockSpec((tm, tn), lambda i, j: (i, j)),
        out_shape=jax.ShapeDtypeStruct((m, n_cols), BF16),
        compiler_params=_params("parallel", "arbitrary"),
        name="proj_" + kind,
    )(u, w_all, *extras)


def _proj_conv_kernel(u_ref, wx_ref, wpre_ref, wpost_ref, wg_ref, cw_ref, cb_ref, o_ref,
                      p_ref, carry_ref, *, tm, tc):
    i = pl.program_id(0)
    j = pl.program_id(1)
    halo = V7X_F32_SUBLANES

    @pl.when(i == 0)
    def _():
        carry_ref[j] = jnp.zeros((halo, tc), F32)

    def proj(w_ref):
        return jnp.dot(u_ref[...], w_ref[...], preferred_element_type=F32)

    p = proj(wpre_ref) * proj(wx_ref)
    p_ref[0:halo, :] = carry_ref[j]
    p_ref[halo:halo + tm, :] = p
    carry_ref[j] = p[tm - halo:tm, :]
    z = (cw_ref[0:1, :] * p_ref[halo - 2:halo - 2 + tm, :]
         + cw_ref[1:2, :] * p_ref[halo - 1:halo - 1 + tm, :]
         + cw_ref[2:3, :] * p) + cb_ref[...]
    gate = jax.nn.silu(proj(wg_ref))
    o_ref[...] = ((proj(wpost_ref) * z) * gate).astype(BF16)


def _proj_conv(u, w_all, layer, col0, c_w, conv_w, conv_b, *, tm):
    m, d = u.shape
    tc = COL_TILE // 4
    nj = c_w // tc
    assert col0 % tc == 0 and c_w % tc == 0

    def w_spec(section):
        c0 = (col0 + section * c_w) // tc
        return pl.BlockSpec((None, d, tc), lambda i, j: (layer, 0, c0 + j))

    kern = functools.partial(_proj_conv_kernel, tm=tm, tc=tc)
    return pl.pallas_call(
        kern,
        grid=(m // tm, nj),
        in_specs=[
            pl.BlockSpec((tm, d), lambda i, j: (i, 0)),
            w_spec(0), w_spec(1), w_spec(2), w_spec(3),
            pl.BlockSpec((CONV_K, tc), lambda i, j: (0, j)),
            pl.BlockSpec((1, tc), lambda i, j: (0, j)),
        ],
        out_specs=pl.BlockSpec((tm, tc), lambda i, j: (i, j)),
        out_shape=jax.ShapeDtypeStruct((m, c_w), BF16),
        scratch_shapes=[pltpu.VMEM((tm + V7X_F32_SUBLANES, tc), F32),
                        pltpu.VMEM((nj, V7X_F32_SUBLANES, tc), F32)],
        compiler_params=_params("arbitrary", "arbitrary"),
        name="proj_conv",
    )(u, w_all, w_all, w_all, w_all, conv_w, conv_b)


def _gate_ret_kernel(cd_ref, u_ref, w_ref, q_ref, k_ref, v_ref, sg_ref, dec_ref, qd_ref, kd_ref,
                     g_ref, yr_ref, s_ref, *, tm, hp, dk, dv, chunks_per_seq):
    i = pl.program_id(0)
    j = pl.program_id(1)

    @pl.when(i == 0)
    def _():
        for hd in range(hp):
            s_ref[j * hp + hd] = jnp.zeros((dk, dv), F32)

    nt = (((1,), (1,)), ((), ()))
    tn = (((0,), (0,)), ((), ()))
    n_chunks = tm // CHUNK
    n_pieces = w_ref.shape[1] // GATE_PIECE
    assert n_pieces > n_chunks
    hs = range(hp)
    kcs = [slice(hd * dk, (hd + 1) * dk) for hd in hs]
    vcs = [slice(hd * dv, (hd + 1) * dv) for hd in hs]

    def gate_piece(pi):
        cols = slice(pi * GATE_PIECE, (pi + 1) * GATE_PIECE)
        acc = jnp.dot(u_ref[...], w_ref[:, cols], preferred_element_type=F32)
        g_ref[:, cols] = jax.nn.sigmoid(acc).astype(BF16)

    def stage_a(c):
        r = slice(c * CHUNK, (c + 1) * CHUNK)
        keep = jnp.where((i * n_chunks + c) % chunks_per_seq == 0, 0.0, 1.0).astype(F32)
        st = [s_ref[j * hp + hd] * keep for hd in hs]
        scores = [lax.dot_general(q_ref[r, kcs[hd]], k_ref[r, kcs[hd]], nt,
                                  preferred_element_type=F32) for hd in hs]
        rhs = [jnp.concatenate([v_ref[r, vcs[hd]], st[hd].astype(BF16)], axis=0) for hd in hs]
        for hd in hs:
            kdec = (k_ref[r, kcs[hd]].astype(F32) * kd_ref[hd]).astype(BF16)
            s_ref[j * hp + hd] = st[hd] * cd_ref[j * hp + hd] + lax.dot_general(
                kdec, v_ref[r, vcs[hd]], tn, preferred_element_type=F32)
        return c, scores, rhs

    def stage_b(c, scores, rhs):
        r = slice(c * CHUNK, (c + 1) * CHUNK)
        for hd in hs:
            lhs = jnp.concatenate(
                [(scores[hd] * dec_ref[hd]).astype(BF16),
                 (q_ref[r, kcs[hd]].astype(F32) * qd_ref[hd]).astype(BF16)], axis=1)
            y = jnp.dot(lhs, rhs[hd], preferred_element_type=F32)
            mu = jnp.mean(y, axis=-1, keepdims=True)
            yc = y - mu
            var = jnp.mean(yc * yc, axis=-1, keepdims=True)
            yn = yc * lax.rsqrt(var + EPS)
            yr_ref[r, vcs[hd]] = (yn * sg_ref[r, vcs[hd]].astype(F32)).astype(BF16)

    pending = None
    for pi in range(n_pieces):
        if pending is not None:
            stage_b(*pending)
        pending = stage_a(pi) if pi < n_chunks else None
        gate_piece(pi)


def _gate_ret(u, w_all, layer, col0, n_cols, qk, v, sg, chunk_decay, decay, q_decay, k_decay, *,
              tm, lp, heads, dk, dv):
    m, d = u.shape
    tn = COL_TILE
    nj = n_cols // tn
    hp = heads // nj
    assert col0 % tn == 0 and heads % nj == 0 and tm % CHUNK == 0 and lp % CHUNK == 0
    c0 = col0 // tn
    kern = functools.partial(_gate_ret_kernel, tm=tm, hp=hp, dk=dk, dv=dv,
                             chunks_per_seq=lp // CHUNK)
    return pl.pallas_call(
        kern,
        grid=(m // tm, nj),
        in_specs=[
            pl.BlockSpec(memory_space=pltpu.SMEM),
            pl.BlockSpec((tm, d), lambda i, j: (i, 0)),
            pl.BlockSpec((None, d, tn), lambda i, j: (layer, 0, c0 + j)),
            pl.BlockSpec((tm, hp * dk), lambda i, j: (i, j)),
            pl.BlockSpec((tm, hp * dk), lambda i, j: (i, nj + j)),
            pl.BlockSpec((tm, hp * dv), lambda i, j: (i, j)),
            pl.BlockSpec((tm, hp * dv), lambda i, j: (i, j)),
            pl.BlockSpec((hp, CHUNK, CHUNK), lambda i, j: (j, 0, 0)),
            pl.BlockSpec((hp, CHUNK, dk), lambda i, j: (j, 0, 0)),
            pl.BlockSpec((hp, CHUNK, dk), lambda i, j: (j, 0, 0)),
        ],
        out_specs=[pl.BlockSpec((tm, tn), lambda i, j: (i, j)),
                   pl.BlockSpec((tm, hp * dv), lambda i, j: (i, j))],
        out_shape=[jax.ShapeDtypeStruct((m, n_cols), BF16),
                   jax.ShapeDtypeStruct((m, heads * dv), BF16)],
        scratch_shapes=[pltpu.VMEM((heads, dk, dv), F32)],
        compiler_params=_params("arbitrary", "arbitrary"),
        name="gate_ret",
    )(chunk_decay, u, w_all, qk, qk, v, sg, decay, q_decay, k_decay)


def _branch_kernel(yr_ref, yc_ref, w0_ref, w1_ref, mr_ref, mc_ref, o_ref):
    a = jnp.dot(yr_ref[...], w0_ref[...], preferred_element_type=F32)
    b = jnp.dot(yc_ref[...], w1_ref[...], preferred_element_type=F32)
    o_ref[...] = (mr_ref[...].astype(F32) * a + mc_ref[...].astype(F32) * b).astype(BF16)


def _branch(yr, yc, w_all, layer, gates, *, tm):
    m, kdim = yr.shape
    n = w_all.shape[-1]
    tn = COL_TILE
    nj = n // tn
    return pl.pallas_call(
        _branch_kernel,
        grid=(m // tm, nj),
        in_specs=[
            pl.BlockSpec((tm, kdim), lambda i, j: (i, 0)),
            pl.BlockSpec((tm, kdim), lambda i, j: (i, 0)),
            pl.BlockSpec((None, None, kdim, tn), lambda i, j: (layer, 0, 0, j)),
            pl.BlockSpec((None, None, kdim, tn), lambda i, j: (layer, 1, 0, j)),
            pl.BlockSpec((tm, tn), lambda i, j: (i, j)),
            pl.BlockSpec((tm, tn), lambda i, j: (i, nj + j)),
        ],
        out_specs=pl.BlockSpec((tm, tn), lambda i, j: (i, j)),
        out_shape=jax.ShapeDtypeStruct((m, n), BF16),
        compiler_params=_params("parallel", "arbitrary"),
        name="branch",
    )(yr, yc, w_all, w_all, gates, gates)


def _out_kernel(a_ref, w_ref, h_ref, o_ref):
    o_ref[...] = h_ref[...] + jnp.dot(a_ref[...], w_ref[...], preferred_element_type=F32)


def _out_norm_kernel(a_ref, w_ref, h_ref, g_ref, o_ref, u_ref):
    hn = h_ref[...] + jnp.dot(a_ref[...], w_ref[...], preferred_element_type=F32)
    o_ref[...] = hn
    u_ref[...] = _rmsnorm(hn, g_ref[...]).astype(BF16)


def _out_proj(a, w_all, layer, h, g_next, *, tm):
    m, kdim = a.shape
    n = w_all.shape[-1]
    row_spec = lambda width: pl.BlockSpec((tm, width), lambda i: (i, 0))
    in_specs = [row_spec(kdim),
                pl.BlockSpec((None, kdim, n), lambda i: (layer, 0, 0),
                             pipeline_mode=pl.Buffered(1)),
                row_spec(n)]
    if g_next is None:
        return pl.pallas_call(
            _out_kernel, grid=(m // tm,), in_specs=in_specs, out_specs=row_spec(n),
            out_shape=jax.ShapeDtypeStruct((m, n), F32),
            compiler_params=_params("parallel"), name="out_proj",
        )(a, w_all, h), None
    return pl.pallas_call(
        _out_norm_kernel, grid=(m // tm,),
        in_specs=in_specs + [pl.BlockSpec((1, n), lambda i: (0, 0))],
        out_specs=[row_spec(n), row_spec(n)],
        out_shape=[jax.ShapeDtypeStruct((m, n), F32), jax.ShapeDtypeStruct((m, n), BF16)],
        compiler_params=_params("parallel"), name="out_proj_norm",
    )(a, w_all, h, g_next)


def _final_norm_kernel(a_ref, b_ref, g_ref, o_ref, *, nc):
    g = g_ref[...]
    for c in range(nc - 1):
        o_ref[0, c] = _rmsnorm(a_ref[0, c + 1], g)
    o_ref[0, nc - 1] = _rmsnorm(b_ref[0, 0], g)


def _final_norm(h4, g, *, nc):
    b, n_chunks, _, d = h4.shape
    kern = functools.partial(_final_norm_kernel, nc=nc)
    return pl.pallas_call(
        kern,
        grid=(b, (n_chunks - 1) // nc),
        in_specs=[
            pl.BlockSpec((1, nc, CHUNK, d), lambda bi, t: (bi, t, 0, 0)),
            pl.BlockSpec((1, 1, CHUNK, d), lambda bi, t: (bi, nc * t + nc, 0, 0)),
            pl.BlockSpec((1, d), lambda bi, t: (0, 0)),
        ],
        out_specs=pl.BlockSpec((1, nc, CHUNK, d), lambda bi, t: (bi, t, 0, 0)),
        out_shape=jax.ShapeDtypeStruct((b, n_chunks - 1, CHUNK, d), F32),
        compiler_params=_params("parallel", "parallel"),
        name="final_norm",
    )(h4, h4, g)


def _tables(lp, front, dk, heads):
    f32 = np.float32
    half = dk // 2
    inv_freq = f32(ROPE_BASE) ** (-np.arange(half, dtype=f32) / f32(half))
    pos = np.maximum(np.arange(lp, dtype=np.int32) - front, 0).astype(f32)
    ang = pos[:, None] * inv_freq[None, :]
    cos = np.cos(ang).astype(f32)
    sin = np.sin(ang).astype(f32)
    cosf = np.concatenate([cos, cos], axis=-1)
    sinf = np.concatenate([-sin, sin], axis=-1)

    log_g = np.log(f32(1.0) - np.exp2(f32(-5.0) - np.arange(heads, dtype=f32))).astype(f32)
    idx = np.arange(CHUNK, dtype=f32)
    diff = idx[:, None] - idx[None, :]
    decay = np.where(diff >= 0, np.exp(log_g[:, None, None] * np.maximum(diff, f32(0.0))),
                     f32(0.0)).astype(f32)
    q_decay = np.broadcast_to(np.exp(log_g[:, None] * (idx + f32(1.0)))[:, :, None],
                              (heads, CHUNK, dk)).astype(f32)
    k_decay = np.broadcast_to(np.exp(log_g[:, None] * (f32(CHUNK - 1.0) - idx))[:, :, None],
                              (heads, CHUNK, dk)).astype(f32)
    chunk_decay = np.exp(log_g * f32(CHUNK)).astype(f32)
    return cosf, sinf, decay, q_decay, k_decay, chunk_decay


def kernel(x, meta_tokens, norm_g, w_in, conv_w, conv_b, gn_g, w_branch, w_out, final_norm_g):
    batch, seq, d = x.shape
    depth = w_in.shape[0]
    heads = RET_HEADS
    dk = d // 16
    dv = 2 * dk
    qk_w = heads * dk
    v_w = heads * dv
    c_w = d
    assert dk == V7X_LANES and seq % (CHUNK * NORM_CHUNKS) == 0

    front = CHUNK - N_META
    lp = front + N_META + seq
    m = batch * lp
    tm_proj = _row_tile(lp, PROJ_ROW_TARGET, V7X_BF16_SUBLANES)
    tm = _row_tile(m, ROW_TILE_TARGET, CHUNK)

    cosf, sinf, decay, q_decay, k_decay, chunk_decay = _tables(lp, front, dk, heads)

    o_v = 2 * qk_w
    o_gr = o_v + v_w
    o_cx = o_gr + v_w
    o_mr = o_cx + 4 * c_w

    w_in_b = w_in.astype(BF16)
    w_branch_b = w_branch.astype(BF16)
    w_out_b = w_out.astype(BF16)

    h, u = _embed(x, meta_tokens.astype(x.dtype), norm_g[0][None, :], front=front)

    for layer in range(depth):
        proj = functools.partial(_proj, u, w_in_b, layer, tm=tm_proj)
        qk = proj(0, 2 * qk_w, "rotary", (cosf, sinf), dk=dk, q_groups=heads)
        v = proj(o_v, v_w, "plain")
        sg = proj(o_gr, v_w, "gain_silu", (gn_g[layer][None, :],))
        yc = _proj_conv(u, w_in_b, layer, o_cx, c_w, conv_w[layer], conv_b[layer][None, :],
                        tm=tm_proj)
        gates, yr = _gate_ret(u, w_in_b, layer, o_mr, 2 * d, qk, v, sg, chunk_decay, decay,
                              q_decay, k_decay, tm=tm, lp=lp, heads=heads, dk=dk, dv=dv)
        merged = _branch(yr, yc, w_branch_b, layer, gates, tm=tm)
        g_next = norm_g[layer + 1][None, :] if layer + 1 < depth else None
        h, u = _out_proj(merged, w_out_b, layer, h, g_next, tm=tm)

    out = _final_norm(h.reshape(batch, lp // CHUNK, CHUNK, d), final_norm_g[None, :],
                      nc=NORM_CHUNKS)
    return out.reshape(batch, seq, d)
```

```python
import functools

import numpy as np
import jax
import jax.numpy as jnp
from jax import lax
from jax.experimental import pallas as pl
from jax.experimental.pallas import tpu as pltpu

F32 = jnp.float32
BF16 = jnp.bfloat16

N_META = 16
RET_HEADS = 8
CONV_K = 3
CHUNK = 128
ROPE_BASE = 10000.0
EPS = 1e-6

V7X_LANES = 128
V7X_F32_SUBLANES = 8
V7X_BF16_SUBLANES = 16
V7X_VMEM_BYTES = 64 * 1024 * 1024
V7X_MXU_COLS = 256

PROJ_ROW_TARGET = 1376
ROW_TILE_TARGET = 768
COL_TILE = 2048
GATE_PIECE = V7X_MXU_COLS
NORM_CHUNKS = 8
VMEM_LIMIT = V7X_VMEM_BYTES - 8 * 1024 * 1024


def _row_tile(m, target, quantum):
    best = quantum
    for t in range(quantum, target + 1, quantum):
        if m % t == 0:
            best = t
    return best


def _params(*sem):
    return pltpu.CompilerParams(dimension_semantics=sem, vmem_limit_bytes=VMEM_LIMIT)


def _rmsnorm(x, g):
    ms = jnp.mean(x * x, axis=-1, keepdims=True)
    return (x * lax.rsqrt(ms + EPS)) * g


def _embed_kernel(x_ref, meta_ref, g_ref, h_ref, u_ref, *, front):
    t = pl.program_id(1)

    @pl.when(t == 0)
    def _():
        d = h_ref.shape[1]
        rows = jnp.concatenate([jnp.zeros((front, d), F32), meta_ref[...]], axis=0)
        h_ref[...] = rows
        u_ref[...] = _rmsnorm(rows, g_ref[...]).astype(BF16)

    @pl.when(t > 0)
    def _():
        rows = x_ref[0]
        h_ref[...] = rows
        u_ref[...] = _rmsnorm(rows, g_ref[...]).astype(BF16)


def _embed(x, meta, g, *, front):
    batch, seq, d = x.shape
    nt = seq // CHUNK + 1
    m = batch * nt * CHUNK
    kern = functools.partial(_embed_kernel, front=front)
    return pl.pallas_call(
        kern,
        grid=(batch, nt),
        in_specs=[
            pl.BlockSpec((1, CHUNK, d), lambda b, t: (b, jnp.maximum(t - 1, 0), 0)),
            pl.BlockSpec((N_META, d), lambda b, t: (0, 0)),
            pl.BlockSpec((1, d), lambda b, t: (0, 0)),
        ],
        out_specs=[
            pl.BlockSpec((CHUNK, d), lambda b, t: (b * nt + t, 0)),
            pl.BlockSpec((CHUNK, d), lambda b, t: (b * nt + t, 0)),
        ],
        out_shape=[jax.ShapeDtypeStruct((m, d), F32), jax.ShapeDtypeStruct((m, d), BF16)],
        compiler_params=_params("parallel", "arbitrary"),
        name="embed",
    )(x, meta, g)


def _proj_kernel(*refs, kind, dk, q_groups, q_scale):
    u_ref, w_ref, o_ref = refs[0], refs[1], refs[-1]
    acc = jnp.dot(u_ref[...], w_ref[...], preferred_element_type=F32)
    if kind == "rotary":
        cs = refs[2][...]
        sn = refs[3][...]
        for gidx in range(acc.shape[1] // dk):
            cols = slice(gidx * dk, (gidx + 1) * dk)
            x = acc[:, cols]
            y = x * cs + pltpu.roll(x, dk // 2, 1) * sn
            if gidx < q_groups:
                y = y * q_scale
            o_ref[:, cols] = y.astype(BF16)
    elif kind == "plain":
        o_ref[...] = acc.astype(BF16)
    else:
        assert kind == "gain_silu"
        o_ref[...] = (jax.nn.silu(acc) * refs[2][...]).astype(BF16)


def _proj(u, w_all, layer, col0, n_cols, kind, extras=(), *, tm, dk=0, q_groups=0):
    m, d = u.shape
    tn = COL_TILE
    assert col0 % tn == 0 and n_cols % tn == 0
    c0 = col0 // tn
    extra_specs = []
    for e in extras:
        if e.shape[0] == 1:
            extra_specs.append(pl.BlockSpec((1, tn), lambda i, j: (0, j)))
        else:
            tiles_per_seq = e.shape[0] // tm
            extra_specs.append(pl.BlockSpec((tm, e.shape[1]),
                                            lambda i, j, t=tiles_per_seq: (i % t, 0)))
    kern = functools.partial(_proj_kernel, kind=kind, dk=dk, q_groups=q_groups,
                             q_scale=dk ** -0.5 if dk else 1.0)
    return pl.pallas_call(
        kern,
        grid=(m // tm, n_cols // tn),
        in_specs=[
            pl.BlockSpec((tm, d), lambda i, j: (i, 0)),
            pl.BlockSpec((None, d, tn), lambda i, j: (layer, 0, c0 + j)),
        ] + extra_specs,
        out_specs=pl.BlockSpec((tm, tn), lambda i, j: (i, j)),
        out_shape=jax.ShapeDtypeStruct((m, n_cols), BF16),
        compiler_params=_params("parallel", "arbitrary"),
        name="proj_" + kind,
    )(u, w_all, *extras)


def _proj_conv_kernel(u_ref, wx_ref, wpre_ref, wpost_ref, wg_ref, cw_ref, cb_ref, o_ref,
                      p_ref, carry_ref, *, tm, tc):
    halo = V7X_F32_SUBLANES

    @pl.when(pl.program_id(1) == 0)
    def _():
        carry_ref[...] = jnp.zeros((halo, tc), F32)

    def proj(w_ref):
        return jnp.dot(u_ref[...], w_ref[...], preferred_element_type=F32)

    p = proj(wpre_ref) * proj(wx_ref)
    p_ref[0:halo, :] = carry_ref[...]
    p_ref[halo:halo + tm, :] = p
    carry_ref[...] = p[tm - halo:tm, :]
    z = (cw_ref[0:1, :] * p_ref[halo - 2:halo - 2 + tm, :]
         + cw_ref[1:2, :] * p_ref[halo - 1:halo - 1 + tm, :]
         + cw_ref[2:3, :] * p) + cb_ref[...]
    gate = jax.nn.silu(proj(wg_ref))
    o_ref[...] = ((proj(wpost_ref) * z) * gate).astype(BF16)


def _proj_conv(u, w_all, layer, col0, c_w, conv_w, conv_b, *, tm):
    m, d = u.shape
    tc = COL_TILE // 4
    nj = c_w // tc
    assert col0 % tc == 0 and c_w % tc == 0

    def w_spec(section):
        c0 = (col0 + section * c_w) // tc
        return pl.BlockSpec((None, d, tc), lambda j, i: (layer, 0, c0 + j))

    kern = functools.partial(_proj_conv_kernel, tm=tm, tc=tc)
    return pl.pallas_call(
        kern,
        grid=(nj, m // tm),
        in_specs=[
            pl.BlockSpec((tm, d), lambda j, i: (i, 0)),
            w_spec(0), w_spec(1), w_spec(2), w_spec(3),
            pl.BlockSpec((CONV_K, tc), lambda j, i: (0, j)),
            pl.BlockSpec((1, tc), lambda j, i: (0, j)),
        ],
        out_specs=pl.BlockSpec((tm, tc), lambda j, i: (i, j)),
        out_shape=jax.ShapeDtypeStruct((m, c_w), BF16),
        scratch_shapes=[pltpu.VMEM((tm + V7X_F32_SUBLANES, tc), F32),
                        pltpu.VMEM((V7X_F32_SUBLANES, tc), F32)],
        compiler_params=_params("arbitrary", "arbitrary"),
        name="proj_conv",
    )(u, w_all, w_all, w_all, w_all, conv_w, conv_b)


def _gate_ret_kernel(cd_ref, u_ref, w_ref, q_ref, k_ref, v_ref, sg_ref, dec_ref, qd_ref, kd_ref,
                     g_ref, yr_ref, s_ref, *, tm, hp, dk, dv, chunks_per_seq):
    j = pl.program_id(0)
    i = pl.program_id(1)

    @pl.when(i == 0)
    def _():
        s_ref[...] = jnp.zeros_like(s_ref)

    nt = (((1,), (1,)), ((), ()))
    tn = (((0,), (0,)), ((), ()))
    n_chunks = tm // CHUNK
    n_pieces = w_ref.shape[1] // GATE_PIECE
    assert n_pieces > n_chunks
    hs = range(hp)
    kcs = [slice(hd * dk, (hd + 1) * dk) for hd in hs]
    vcs = [slice(hd * dv, (hd + 1) * dv) for hd in hs]

    def gate_piece(pi):
        cols = slice(pi * GATE_PIECE, (pi + 1) * GATE_PIECE)
        acc = jnp.dot(u_ref[...], w_ref[:, cols], preferred_element_type=F32)
        g_ref[:, cols] = jax.nn.sigmoid(acc).astype(BF16)

    def stage_a(c):
        r = slice(c * CHUNK, (c + 1) * CHUNK)
        keep = jnp.where((i * n_chunks + c) % chunks_per_seq == 0, 0.0, 1.0).astype(F32)
        st = [s_ref[hd] * keep for hd in hs]
        scores = [lax.dot_general(q_ref[r, kcs[hd]], k_ref[r, kcs[hd]], nt,
                                  preferred_element_type=F32) for hd in hs]
        rhs = [jnp.concatenate([v_ref[r, vcs[hd]], st[hd].astype(BF16)], axis=0) for hd in hs]
        for hd in hs:
            kdec = (k_ref[r, kcs[hd]].astype(F32) * kd_ref[hd]).astype(BF16)
            s_ref[hd] = st[hd] * cd_ref[j * hp + hd] + lax.dot_general(
                kdec, v_ref[r, vcs[hd]], tn, preferred_element_type=F32)
        return c, scores, rhs

    def stage_b(c, scores, rhs):
        r = slice(c * CHUNK, (c + 1) * CHUNK)
        for hd in hs:
            lhs = jnp.concatenate(
                [(scores[hd] * dec_ref[hd]).astype(BF16),
                 (q_ref[r, kcs[hd]].astype(F32) * qd_ref[hd]).astype(BF16)], axis=1)
            y = jnp.dot(lhs, rhs[hd], preferred_element_type=F32)
            mu = jnp.mean(y, axis=-1, keepdims=True)
            yc = y - mu
            var = jnp.mean(yc * yc, axis=-1, keepdims=True)
            yn = yc * lax.rsqrt(var + EPS)
            yr_ref[r, vcs[hd]] = (yn * sg_ref[r, vcs[hd]].astype(F32)).astype(BF16)

    pending = None
    for pi in range(n_pieces):
        if pending is not None:
            stage_b(*pending)
        pending = stage_a(pi) if pi < n_chunks else None
        gate_piece(pi)


def _gate_ret(u, w_all, layer, col0, n_cols, qk, v, sg, chunk_decay, decay, q_decay, k_decay, *,
              tm, lp, heads, dk, dv):
    m, d = u.shape
    tn = COL_TILE
    nj = n_cols // tn
    hp = heads // nj
    assert col0 % tn == 0 and heads % nj == 0 and tm % CHUNK == 0 and lp % CHUNK == 0
    c0 = col0 // tn
    kern = functools.partial(_gate_ret_kernel, tm=tm, hp=hp, dk=dk, dv=dv,
                             chunks_per_seq=lp // CHUNK)
    return pl.pallas_call(
        kern,
        grid=(nj, m // tm),
        in_specs=[
            pl.BlockSpec(memory_space=pltpu.SMEM),
            pl.BlockSpec((tm, d), lambda j, i: (i, 0)),
            pl.BlockSpec((None, d, tn), lambda j, i: (layer, 0, c0 + j)),
            pl.BlockSpec((tm, hp * dk), lambda j, i: (i, j)),
            pl.BlockSpec((tm, hp * dk), lambda j, i: (i, nj + j)),
            pl.BlockSpec((tm, hp * dv), lambda j, i: (i, j)),
            pl.BlockSpec((tm, hp * dv), lambda j, i: (i, j)),
            pl.BlockSpec((hp, CHUNK, CHUNK), lambda j, i: (j, 0, 0)),
            pl.BlockSpec((hp, CHUNK, dk), lambda j, i: (j, 0, 0)),
            pl.BlockSpec((hp, CHUNK, dk), lambda j, i: (j, 0, 0)),
        ],
        out_specs=[pl.BlockSpec((tm, tn), lambda j, i: (i, j)),
                   pl.BlockSpec((tm, hp * dv), lambda j, i: (i, j))],
        out_shape=[jax.ShapeDtypeStruct((m, n_cols), BF16),
                   jax.ShapeDtypeStruct((m, heads * dv), BF16)],
        scratch_shapes=[pltpu.VMEM((hp, dk, dv), F32)],
        compiler_params=_params("arbitrary", "arbitrary"),
        name="gate_ret",
    )(chunk_decay, u, w_all, qk, qk, v, sg, decay, q_decay, k_decay)


def _branch_kernel(yr_ref, yc_ref, w0_ref, w1_ref, mr_ref, mc_ref, o_ref):
    a = jnp.dot(yr_ref[...], w0_ref[...], preferred_element_type=F32)
    b = jnp.dot(yc_ref[...], w1_ref[...], preferred_element_type=F32)
    o_ref[...] = (mr_ref[...].astype(F32) * a + mc_ref[...].astype(F32) * b).astype(BF16)


def _branch(yr, yc, w_all, layer, gates, *, tm):
    m, kdim = yr.shape
    n = w_all.shape[-1]
    tn = COL_TILE
    nj = n // tn
    return pl.pallas_call(
        _branch_kernel,
        grid=(m // tm, nj),
        in_specs=[
            pl.BlockSpec((tm, kdim), lambda i, j: (i, 0)),
            pl.BlockSpec((tm, kdim), lambda i, j: (i, 0)),
            pl.BlockSpec((None, None, kdim, tn), lambda i, j: (layer, 0, 0, j)),
            pl.BlockSpec((None, None, kdim, tn), lambda i, j: (layer, 1, 0, j)),
            pl.BlockSpec((tm, tn), lambda i, j: (i, j)),
            pl.BlockSpec((tm, tn), lambda i, j: (i, nj + j)),
        ],
        out_specs=pl.BlockSpec((tm, tn), lambda i, j: (i, j)),
        out_shape=jax.ShapeDtypeStruct((m, n), BF16),
        compiler_params=_params("parallel", "arbitrary"),
        name="branch",
    )(yr, yc, w_all, w_all, gates, gates)


def _out_kernel(a_ref, w_ref, h_ref, o_ref):
    o_ref[...] = h_ref[...] + jnp.dot(a_ref[...], w_ref[...], preferred_element_type=F32)


def _out_norm_kernel(a_ref, w_ref, h_ref, g_ref, o_ref, u_ref):
    hn = h_ref[...] + jnp.dot(a_ref[...], w_ref[...], preferred_element_type=F32)
    o_ref[...] = hn
    u_ref[...] = _rmsnorm(hn, g_ref[...]).astype(BF16)


def _out_proj(a, w_all, layer, h, g_next, *, tm):
    m, kdim = a.shape
    n = w_all.shape[-1]
    row_spec = lambda width: pl.BlockSpec((tm, width), lambda i: (i, 0))
    in_specs = [row_spec(kdim),
                pl.BlockSpec((None, kdim, n), lambda i: (layer, 0, 0),
                             pipeline_mode=pl.Buffered(1)),
                row_spec(n)]
    if g_next is None:
        return pl.pallas_call(
            _out_kernel, grid=(m // tm,), in_specs=in_specs, out_specs=row_spec(n),
            out_shape=jax.ShapeDtypeStruct((m, n), F32),
            compiler_params=_params("parallel"), name="out_proj",
        )(a, w_all, h), None
    return pl.pallas_call(
        _out_norm_kernel, grid=(m // tm,),
        in_specs=in_specs + [pl.BlockSpec((1, n), lambda i: (0, 0))],
        out_specs=[row_spec(n), row_spec(n)],
        out_shape=[jax.ShapeDtypeStruct((m, n), F32), jax.ShapeDtypeStruct((m, n), BF16)],
        compiler_params=_params("parallel"), name="out_proj_norm",
    )(a, w_all, h, g_next)


def _final_norm_kernel(a_ref, b_ref, g_ref, o_ref, *, nc):
    g = g_ref[...]
    for c in range(nc - 1):
        o_ref[0, c] = _rmsnorm(a_ref[0, c + 1], g)
    o_ref[0, nc - 1] = _rmsnorm(b_ref[0, 0], g)


def _final_norm(h4, g, *, nc):
    b, n_chunks, _, d = h4.shape
    kern = functools.partial(_final_norm_kernel, nc=nc)
    return pl.pallas_call(
        kern,
        grid=(b, (n_chunks - 1) // nc),
        in_specs=[
            pl.BlockSpec((1, nc, CHUNK, d), lambda bi, t: (bi, t, 0, 0)),
            pl.BlockSpec((1, 1, CHUNK, d), lambda bi, t: (bi, nc * t + nc, 0, 0)),
            pl.BlockSpec((1, d), lambda bi, t: (0, 0)),
        ],
        out_specs=pl.BlockSpec((1, nc, CHUNK, d), lambda bi, t: (bi, t, 0, 0)),
        out_shape=jax.ShapeDtypeStruct((b, n_chunks - 1, CHUNK, d), F32),
        compiler_params=_params("parallel", "parallel"),
        name="final_norm",
    )(h4, h4, g)


def _tables(lp, front, dk, heads):
    f32 = np.float32
    half = dk // 2
    inv_freq = f32(ROPE_BASE) ** (-np.arange(half, dtype=f32) / f32(half))
    pos = np.maximum(np.arange(lp, dtype=np.int32) - front, 0).astype(f32)
    ang = pos[:, None] * inv_freq[None, :]
    cos = np.cos(ang).astype(f32)
    sin = np.sin(ang).astype(f32)
    cosf = np.concatenate([cos, cos], axis=-1)
    sinf = np.concatenate([-sin, sin], axis=-1)

    log_g = np.log(f32(1.0) - np.exp2(f32(-5.0) - np.arange(heads, dtype=f32))).astype(f32)
    idx = np.arange(CHUNK, dtype=f32)
    diff = idx[:, None] - idx[None, :]
    decay = np.where(diff >= 0, np.exp(log_g[:, None, None] * np.maximum(diff, f32(0.0))),
                     f32(0.0)).astype(f32)
    q_decay = np.broadcast_to(np.exp(log_g[:, None] * (idx + f32(1.0)))[:, :, None],
                              (heads, CHUNK, dk)).astype(f32)
    k_decay = np.broadcast_to(np.exp(log_g[:, None] * (f32(CHUNK - 1.0) - idx))[:, :, None],
                              (heads, CHUNK, dk)).astype(f32)
    chunk_decay = np.exp(log_g * f32(CHUNK)).astype(f32)
    return cosf, sinf, decay, q_decay, k_decay, chunk_decay


def kernel(x, meta_tokens, norm_g, w_in, conv_w, conv_b, gn_g, w_branch, w_out, final_norm_g):
    batch, seq, d = x.shape
    depth = w_in.shape[0]
    heads = RET_HEADS
    dk = d // 16
    dv = 2 * dk
    qk_w = heads * dk
    v_w = heads * dv
    c_w = d
    assert dk == V7X_LANES and seq % (CHUNK * NORM_CHUNKS) == 0

    front = CHUNK - N_META
    lp = front + N_META + seq
    m = batch * lp
    tm_proj = _row_tile(lp, PROJ_ROW_TARGET, V7X_BF16_SUBLANES)
    tm = _row_tile(m, ROW_TILE_TARGET, CHUNK)

    cosf, sinf, decay, q_decay, k_decay, chunk_decay = _tables(lp, front, dk, heads)

    o_v = 2 * qk_w
    o_gr = o_v + v_w
    o_cx = o_gr + v_w
    o_mr = o_cx + 4 * c_w

    w_in_b = w_in.astype(BF16)
    w_branch_b = w_branch.astype(BF16)
    w_out_b = w_out.astype(BF16)

    h, u = _embed(x, meta_tokens.astype(x.dtype), norm_g[0][None, :], front=front)

    for layer in range(depth):
        proj = functools.partial(_proj, u, w_in_b, layer, tm=tm_proj)
        qk = proj(0, 2 * qk_w, "rotary", (cosf, sinf), dk=dk, q_groups=heads)
        v = proj(o_v, v_w, "plain")
        sg = proj(o_gr, v_w, "gain_silu", (gn_g[layer][None, :],))
        yc = _proj_conv(u, w_in_b, layer, o_cx, c_w, conv_w[layer], conv_b[layer][None, :],
                        tm=tm_proj)
        gates, yr = _gate_ret(u, w_in_b, layer, o_mr, 2 * d, qk, v, sg, chunk_decay, decay,
                              q_decay, k_decay, tm=tm, lp=lp, heads=heads, dk=dk, dv=dv)
        merged = _branch(yr, yc, w_branch_b, layer, gates, tm=tm)
        g_next = norm_g[layer + 1][None, :] if layer + 1 < depth else None
        h, u = _out_proj(merged, w_out_b, layer, h, g_next, tm=tm)

    out = _final_norm(h.reshape(batch, lp // CHUNK, CHUNK, d), final_norm_g[None, :],
                      nc=NORM_CHUNKS)
    return out.reshape(batch, seq, d)
```

```python
import functools

import numpy as np
import jax
import jax.numpy as jnp
from jax import lax
from jax.experimental import pallas as pl
from jax.experimental.pallas import tpu as pltpu

F32 = jnp.float32
BF16 = jnp.bfloat16

N_META = 16
RET_HEADS = 8
CONV_K = 3
CHUNK = 128
ROPE_BASE = 10000.0
EPS = 1e-6

V7X_LANES = 128
V7X_F32_SUBLANES = 8
V7X_BF16_SUBLANES = 16
V7X_VMEM_BYTES = 64 * 1024 * 1024
V7X_MXU_COLS = 256

PROJ_ROW_TARGET = 1376
ROW_TILE_TARGET = 768
COL_TILE = 2048
GATE_PIECE = V7X_MXU_COLS
NORM_CHUNKS = 8
VMEM_LIMIT = V7X_VMEM_BYTES - 8 * 1024 * 1024


def _row_tile(m, target, quantum):
    best = quantum
    for t in range(quantum, target + 1, quantum):
        if m % t == 0:
            best = t
    return best


def _params(*sem):
    return pltpu.CompilerParams(dimension_semantics=sem, vmem_limit_bytes=VMEM_LIMIT)


def _rmsnorm(x, g):
    ms = jnp.mean(x * x, axis=-1, keepdims=True)
    return (x * lax.rsqrt(ms + EPS)) * g


def _embed_kernel(xa_ref, xb_ref, meta_ref, g_ref, h_ref, u_ref, *, front, nc):
    g = g_ref[...]
    d = g.shape[1]
    lead = jnp.concatenate([jnp.zeros((front, d), F32), meta_ref[...]], axis=0)
    first = jnp.where(pl.program_id(1) == 0, lead, xb_ref[0, 0])
    h_ref[0, 0] = first
    u_ref[0, 0] = _rmsnorm(first, g).astype(BF16)
    for c in range(1, nc):
        rows = xa_ref[0, c - 1]
        h_ref[0, c] = rows
        u_ref[0, c] = _rmsnorm(rows, g).astype(BF16)


def _embed_tail_kernel(x_ref, g_ref, h_in_ref, u_in_ref, h_ref, u_ref):
    del h_in_ref, u_in_ref
    rows = x_ref[0, 0]
    h_ref[0, 0] = rows
    u_ref[0, 0] = _rmsnorm(rows, g_ref[...]).astype(BF16)


def _embed(x, meta, g, *, front, nc):
    batch, seq, d = x.shape
    n_in = seq // CHUNK
    x4 = x.reshape(batch, n_in, CHUNK, d)
    shapes = [jax.ShapeDtypeStruct((batch, n_in + 1, CHUNK, d), F32),
              jax.ShapeDtypeStruct((batch, n_in + 1, CHUNK, d), BF16)]
    block = lambda n: (1, n, CHUNK, d)
    kern = functools.partial(_embed_kernel, front=front, nc=nc)
    h4, u4 = pl.pallas_call(
        kern,
        grid=(batch, n_in // nc),
        in_specs=[
            pl.BlockSpec(block(nc), lambda b, t: (b, t, 0, 0)),
            pl.BlockSpec(block(1), lambda b, t: (b, jnp.maximum(nc * t - 1, 0), 0, 0)),
            pl.BlockSpec((N_META, d), lambda b, t: (0, 0)),
            pl.BlockSpec((1, d), lambda b, t: (0, 0)),
        ],
        out_specs=[pl.BlockSpec(block(nc), lambda b, t: (b, t, 0, 0)),
                   pl.BlockSpec(block(nc), lambda b, t: (b, t, 0, 0))],
        out_shape=shapes,
        compiler_params=_params("parallel", "parallel"),
        name="embed",
    )(x4, x4, meta, g)
    h4, u4 = pl.pallas_call(
        _embed_tail_kernel,
        grid=(batch,),
        in_specs=[
            pl.BlockSpec(block(1), lambda b: (b, n_in - 1, 0, 0)),
            pl.BlockSpec((1, d), lambda b: (0, 0)),
            pl.BlockSpec(memory_space=pl.ANY),
            pl.BlockSpec(memory_space=pl.ANY),
        ],
        out_specs=[pl.BlockSpec(block(1), lambda b: (b, n_in, 0, 0)),
                   pl.BlockSpec(block(1), lambda b: (b, n_in, 0, 0))],
        out_shape=shapes,
        input_output_aliases={2: 0, 3: 1},
        compiler_params=_params("parallel"),
        name="embed_tail",
    )(x4, g, h4, u4)
    m = batch * (n_in + 1) * CHUNK
    return h4.reshape(m, d), u4.reshape(m, d)


def _proj_kernel(*refs, kind, dk, q_groups, q_scale):
    u_ref, w_ref, o_ref = refs[0], refs[1], refs[-1]
    acc = jnp.dot(u_ref[...], w_ref[...], preferred_element_type=F32)
    if kind == "rotary":
        cs = refs[2][...]
        sn = refs[3][...]
        for gidx in range(acc.shape[1] // dk):
            cols = slice(gidx * dk, (gidx + 1) * dk)
            x = acc[:, cols]
            y = x * cs + pltpu.roll(x, dk // 2, 1) * sn
            if gidx < q_groups:
                y = y * q_scale
            o_ref[:, cols] = y.astype(BF16)
    elif kind == "plain":
        o_ref[...] = acc.astype(BF16)
    else:
        assert kind == "gain_silu"
        o_ref[...] = (jax.nn.silu(acc) * refs[2][...]).astype(BF16)


def _proj(u, w_all, layer, col0, n_cols, kind, extras=(), *, tm, dk=0, q_groups=0):
    m, d = u.shape
    tn = COL_TILE
    assert col0 % tn == 0 and n_cols % tn == 0
    c0 = col0 // tn
    extra_specs = []
    for e in extras:
        if e.shape[0] == 1:
            extra_specs.append(pl.BlockSpec((1, tn), lambda i, j: (0, j)))
        else:
            tiles_per_seq = e.shape[0] // tm
            extra_specs.append(pl.BlockSpec((tm, e.shape[1]),
                                            lambda i, j, t=tiles_per_seq: (i % t, 0)))
    kern = functools.partial(_proj_kernel, kind=kind, dk=dk, q_groups=q_groups,
                             q_scale=dk ** -0.5 if dk else 1.0)
    return pl.pallas_call(
        kern,
        grid=(m // tm, n_cols // tn),
        in_specs=[
            pl.BlockSpec((tm, d), lambda i, j: (i, 0)),
            pl.BlockSpec((None, d, tn), lambda i, j: (layer, 0, c0 + j)),
        ] + extra_specs,
        out_specs=pl.BlockSpec((tm, tn), lambda i, j: (i, j)),
        out_shape=jax.ShapeDtypeStruct((m, n_cols), BF16),
        compiler_params=_params("parallel", "arbitrary"),
        name="proj_" + kind,
    )(u, w_all, *extras)


def _proj_conv_kernel(u_ref, wx_ref, wpre_ref, wpost_ref, wg_ref, cw_ref, cb_ref, o_ref,
                      p_ref, carry_ref, *, tm, tc):
    halo = V7X_F32_SUBLANES

    @pl.when(pl.program_id(1) == 0)
    def _():
        carry_ref[...] = jnp.zeros((halo, tc), F32)

    def proj(w_ref):
        return jnp.dot(u_ref[...], w_ref[...], preferred_element_type=F32)

    p = proj(wpre_ref) * proj(wx_ref)
    p_ref[0:halo, :] = carry_ref[...]
    p_ref[halo:halo + tm, :] = p
    carry_ref[...] = p[tm - halo:tm, :]
    z = (cw_ref[0:1, :] * p_ref[halo - 2:halo - 2 + tm, :]
         + cw_ref[1:2, :] * p_ref[halo - 1:halo - 1 + tm, :]
         + cw_ref[2:3, :] * p) + cb_ref[...]
    gate = jax.nn.silu(proj(wg_ref))
    o_ref[...] = ((proj(wpost_ref) * z) * gate).astype(BF16)


def _proj_conv(u, w_all, layer, col0, c_w, conv_w, conv_b, *, tm):
    m, d = u.shape
    tc = COL_TILE // 4
    nj = c_w // tc
    assert col0 % tc == 0 and c_w % tc == 0

    def w_spec(section):
        c0 = (col0 + section * c_w) // tc
        return pl.BlockSpec((None, d, tc), lambda j, i: (layer, 0, c0 + j))

    kern = functools.partial(_proj_conv_kernel, tm=tm, tc=tc)
    return pl.pallas_call(
        kern,
        grid=(nj, m // tm),
        in_specs=[
            pl.BlockSpec((tm, d), lambda j, i: (i, 0)),
            w_spec(0), w_spec(1), w_spec(2), w_spec(3),
            pl.BlockSpec((CONV_K, tc), lambda j, i: (0, j)),
            pl.BlockSpec((1, tc), lambda j, i: (0, j)),
        ],
        out_specs=pl.BlockSpec((tm, tc), lambda j, i: (i, j)),
        out_shape=jax.ShapeDtypeStruct((m, c_w), BF16),
        scratch_shapes=[pltpu.VMEM((tm + V7X_F32_SUBLANES, tc), F32),
                        pltpu.VMEM((V7X_F32_SUBLANES, tc), F32)],
        compiler_params=_params("arbitrary", "arbitrary"),
        name="proj_conv",
    )(u, w_all, w_all, w_all, w_all, conv_w, conv_b)


def _gate_ret_kernel(cd_ref, u_ref, w_ref, q_ref, k_ref, v_ref, sg_ref, dec_ref, qd_ref, kd_ref,
                     g_ref, yr_ref, s_ref, *, tm, hp, dk, dv, chunks_per_seq):
    j = pl.program_id(0)
    i = pl.program_id(1)

    @pl.when(i == 0)
    def _():
        s_ref[...] = jnp.zeros_like(s_ref)

    nt = (((1,), (1,)), ((), ()))
    tn = (((0,), (0,)), ((), ()))
    n_chunks = tm // CHUNK
    n_pieces = w_ref.shape[1] // GATE_PIECE
    assert n_pieces > n_chunks
    hs = range(hp)
    kcs = [slice(hd * dk, (hd + 1) * dk) for hd in hs]
    vcs = [slice(hd * dv, (hd + 1) * dv) for hd in hs]

    def gate_piece(pi):
        cols = slice(pi * GATE_PIECE, (pi + 1) * GATE_PIECE)
        acc = jnp.dot(u_ref[...], w_ref[:, cols], preferred_element_type=F32)
        g_ref[:, cols] = jax.nn.sigmoid(acc).astype(BF16)

    def stage_a(c):
        r = slice(c * CHUNK, (c + 1) * CHUNK)
        keep = jnp.where((i * n_chunks + c) % chunks_per_seq == 0, 0.0, 1.0).astype(F32)
        st = [s_ref[hd] * keep for hd in hs]
        scores = [lax.dot_general(q_ref[r, kcs[hd]], k_ref[r, kcs[hd]], nt,
                                  preferred_element_type=F32) for hd in hs]
        rhs = [jnp.concatenate([v_ref[r, vcs[hd]], st[hd].astype(BF16)], axis=0) for hd in hs]
        for hd in hs:
            kdec = (k_ref[r, kcs[hd]].astype(F32) * kd_ref[hd]).astype(BF16)
            s_ref[hd] = st[hd] * cd_ref[j * hp + hd] + lax.dot_general(
                kdec, v_ref[r, vcs[hd]], tn, preferred_element_type=F32)
        return c, scores, rhs

    def stage_b(c, scores, rhs):
        r = slice(c * CHUNK, (c + 1) * CHUNK)
        for hd in hs:
            lhs = jnp.concatenate(
                [(scores[hd] * dec_ref[hd]).astype(BF16),
                 (q_ref[r, kcs[hd]].astype(F32) * qd_ref[hd]).astype(BF16)], axis=1)
            y = jnp.dot(lhs, rhs[hd], preferred_element_type=F32)
            mu = jnp.mean(y, axis=-1, keepdims=True)
            yc = y - mu
            var = jnp.mean(yc * yc, axis=-1, keepdims=True)
            yn = yc * lax.rsqrt(var + EPS)
            yr_ref[r, vcs[hd]] = (yn * sg_ref[r, vcs[hd]].astype(F32)).astype(BF16)

    pending = None
    for pi in range(n_pieces):
        if pending is not None:
            stage_b(*pending)
        pending = stage_a(pi) if pi < n_chunks else None
        gate_piece(pi)


def _gate_ret(u, w_all, layer, col0, n_cols, qk, v, sg, chunk_decay, decay, q_decay, k_decay, *,
              tm, lp, heads, dk, dv):
    m, d = u.shape
    tn = COL_TILE
    nj = n_cols // tn
    hp = heads // nj
    assert col0 % tn == 0 and heads % nj == 0 and tm % CHUNK == 0 and lp % CHUNK == 0
    c0 = col0 // tn
    kern = functools.partial(_gate_ret_kernel, tm=tm, hp=hp, dk=dk, dv=dv,
                             chunks_per_seq=lp // CHUNK)
    return pl.pallas_call(
        kern,
        grid=(nj, m // tm),
        in_specs=[
            pl.BlockSpec(memory_space=pltpu.SMEM),
            pl.BlockSpec((tm, d), lambda j, i: (i, 0)),
            pl.BlockSpec((None, d, tn), lambda j, i: (layer, 0, c0 + j)),
            pl.BlockSpec((tm, hp * dk), lambda j, i: (i, j)),
            pl.BlockSpec((tm, hp * dk), lambda j, i: (i, nj + j)),
            pl.BlockSpec((tm, hp * dv), lambda j, i: (i, j)),
            pl.BlockSpec((tm, hp * dv), lambda j, i: (i, j)),
            pl.BlockSpec((hp, CHUNK, CHUNK), lambda j, i: (j, 0, 0)),
            pl.BlockSpec((hp, CHUNK, dk), lambda j, i: (j, 0, 0)),
            pl.BlockSpec((hp, CHUNK, dk), lambda j, i: (j, 0, 0)),
        ],
        out_specs=[pl.BlockSpec((tm, tn), lambda j, i: (i, j)),
                   pl.BlockSpec((tm, hp * dv), lambda j, i: (i, j))],
        out_shape=[jax.ShapeDtypeStruct((m, n_cols), BF16),
                   jax.ShapeDtypeStruct((m, heads * dv), BF16)],
        scratch_shapes=[pltpu.VMEM((hp, dk, dv), F32)],
        compiler_params=_params("arbitrary", "arbitrary"),
        name="gate_ret",
    )(chunk_decay, u, w_all, qk, qk, v, sg, decay, q_decay, k_decay)


def _branch_kernel(yr_ref, yc_ref, w0_ref, w1_ref, mr_ref, mc_ref, o_ref):
    a = jnp.dot(yr_ref[...], w0_ref[...], preferred_element_type=F32)
    b = jnp.dot(yc_ref[...], w1_ref[...], preferred_element_type=F32)
    o_ref[...] = (mr_ref[...].astype(F32) * a + mc_ref[...].astype(F32) * b).astype(BF16)


def _branch(yr, yc, w_all, layer, gates, *, tm):
    m, kdim = yr.shape
    n = w_all.shape[-1]
    tn = COL_TILE
    nj = n // tn
    return pl.pallas_call(
        _branch_kernel,
        grid=(m // tm, nj),
        in_specs=[
            pl.BlockSpec((tm, kdim), lambda i, j: (i, 0)),
            pl.BlockSpec((tm, kdim), lambda i, j: (i, 0)),
            pl.BlockSpec((None, None, kdim, tn), lambda i, j: (layer, 0, 0, j)),
            pl.BlockSpec((None, None, kdim, tn), lambda i, j: (layer, 1, 0, j)),
            pl.BlockSpec((tm, tn), lambda i, j: (i, j)),
            pl.BlockSpec((tm, tn), lambda i, j: (i, nj + j)),
        ],
        out_specs=pl.BlockSpec((tm, tn), lambda i, j: (i, j)),
        out_shape=jax.ShapeDtypeStruct((m, n), BF16),
        compiler_params=_params("parallel", "arbitrary"),
        name="branch",
    )(yr, yc, w_all, w_all, gates, gates)


def _out_kernel(a_ref, w_ref, h_ref, o_ref):
    o_ref[...] = h_ref[...] + jnp.dot(a_ref[...], w_ref[...], preferred_element_type=F32)


def _out_norm_kernel(a_ref, w_ref, h_ref, g_ref, o_ref, u_ref):
    hn = h_ref[...] + jnp.dot(a_ref[...], w_ref[...], preferred_element_type=F32)
    o_ref[...] = hn
    u_ref[...] = _rmsnorm(hn, g_ref[...]).astype(BF16)


def _out_proj(a, w_all, layer, h, g_next, *, tm):
    m, kdim = a.shape
    n = w_all.shape[-1]
    row_spec = lambda width: pl.BlockSpec((tm, width), lambda i: (i, 0))
    in_specs = [row_spec(kdim),
                pl.BlockSpec((None, kdim, n), lambda i: (layer, 0, 0),
                             pipeline_mode=pl.Buffered(1)),
                row_spec(n)]
    if g_next is None:
        return pl.pallas_call(
            _out_kernel, grid=(m // tm,), in_specs=in_specs, out_specs=row_spec(n),
            out_shape=jax.ShapeDtypeStruct((m, n), F32),
            compiler_params=_params("parallel"), name="out_proj",
        )(a, w_all, h), None
    return pl.pallas_call(
        _out_norm_kernel, grid=(m // tm,),
        in_specs=in_specs + [pl.BlockSpec((1, n), lambda i: (0, 0))],
        out_specs=[row_spec(n), row_spec(n)],
        out_shape=[jax.ShapeDtypeStruct((m, n), F32), jax.ShapeDtypeStruct((m, n), BF16)],
        compiler_params=_params("parallel"), name="out_proj_norm",
    )(a, w_all, h, g_next)


def _final_norm_kernel(a_ref, b_ref, g_ref, o_ref, *, nc):
    g = g_ref[...]
    for c in range(nc - 1):
        o_ref[0, c] = _rmsnorm(a_ref[0, c + 1], g)
    o_ref[0, nc - 1] = _rmsnorm(b_ref[0, 0], g)


def _final_norm(h4, g, *, nc):
    b, n_chunks, _, d = h4.shape
    kern = functools.partial(_final_norm_kernel, nc=nc)
    return pl.pallas_call(
        kern,
        grid=(b, (n_chunks - 1) // nc),
        in_specs=[
            pl.BlockSpec((1, nc, CHUNK, d), lambda bi, t: (bi, t, 0, 0)),
            pl.BlockSpec((1, 1, CHUNK, d), lambda bi, t: (bi, nc * t + nc, 0, 0)),
            pl.BlockSpec((1, d), lambda bi, t: (0, 0)),
        ],
        out_specs=pl.BlockSpec((1, nc, CHUNK, d), lambda bi, t: (bi, t, 0, 0)),
        out_shape=jax.ShapeDtypeStruct((b, n_chunks - 1, CHUNK, d), F32),
        compiler_params=_params("parallel", "parallel"),
        name="final_norm",
    )(h4, h4, g)


def _tables(lp, front, dk, heads):
    f32 = np.float32
    half = dk // 2
    inv_freq = f32(ROPE_BASE) ** (-np.arange(half, dtype=f32) / f32(half))
    pos = np.maximum(np.arange(lp, dtype=np.int32) - front, 0).astype(f32)
    ang = pos[:, None] * inv_freq[None, :]
    cos = np.cos(ang).astype(f32)
    sin = np.sin(ang).astype(f32)
    cosf = np.concatenate([cos, cos], axis=-1)
    sinf = np.concatenate([-sin, sin], axis=-1)

    log_g = np.log(f32(1.0) - np.exp2(f32(-5.0) - np.arange(heads, dtype=f32))).astype(f32)
    idx = np.arange(CHUNK, dtype=f32)
    diff = idx[:, None] - idx[None, :]
    decay = np.where(diff >= 0, np.exp(log_g[:, None, None] * np.maximum(diff, f32(0.0))),
                     f32(0.0)).astype(f32)
    q_decay = np.broadcast_to(np.exp(log_g[:, None] * (idx + f32(1.0)))[:, :, None],
                              (heads, CHUNK, dk)).astype(f32)
    k_decay = np.broadcast_to(np.exp(log_g[:, None] * (f32(CHUNK - 1.0) - idx))[:, :, None],
                              (heads, CHUNK, dk)).astype(f32)
    chunk_decay = np.exp(log_g * f32(CHUNK)).astype(f32)
    return cosf, sinf, decay, q_decay, k_decay, chunk_decay


def kernel(x, meta_tokens, norm_g, w_in, conv_w, conv_b, gn_g, w_branch, w_out, final_norm_g):
    batch, seq, d = x.shape
    depth = w_in.shape[0]
    heads = RET_HEADS
    dk = d // 16
    dv = 2 * dk
    qk_w = heads * dk
    v_w = heads * dv
    c_w = d
    assert dk == V7X_LANES and seq % (CHUNK * NORM_CHUNKS) == 0

    front = CHUNK - N_META
    lp = front + N_META + seq
    m = batch * lp
    tm_proj = _row_tile(lp, PROJ_ROW_TARGET, V7X_BF16_SUBLANES)
    tm = _row_tile(m, ROW_TILE_TARGET, CHUNK)

    cosf, sinf, decay, q_decay, k_decay, chunk_decay = _tables(lp, front, dk, heads)

    o_v = 2 * qk_w
    o_gr = o_v + v_w
    o_cx = o_gr + v_w
    o_mr = o_cx + 4 * c_w

    w_in_b = w_in.astype(BF16)
    w_branch_b = w_branch.astype(BF16)
    w_out_b = w_out.astype(BF16)

    h, u = _embed(x, meta_tokens.astype(x.dtype), norm_g[0][None, :], front=front,
                  nc=NORM_CHUNKS)

    for layer in range(depth):
        proj = functools.partial(_proj, u, w_in_b, layer, tm=tm_proj)
        qk = proj(0, 2 * qk_w, "rotary", (cosf, sinf), dk=dk, q_groups=heads)
        v = proj(o_v, v_w, "plain")
        sg = proj(o_gr, v_w, "gain_silu", (gn_g[layer][None, :],))
        yc = _proj_conv(u, w_in_b, layer, o_cx, c_w, conv_w[layer], conv_b[layer][None, :],
                        tm=tm_proj)
        gates, yr = _gate_ret(u, w_in_b, layer, o_mr, 2 * d, qk, v, sg, chunk_decay, decay,
                              q_decay, k_decay, tm=tm, lp=lp, heads=heads, dk=dk, dv=dv)
        merged = _branch(yr, yc, w_branch_b, layer, gates, tm=tm)
        g_next = norm_g[layer + 1][None, :] if layer + 1 < depth else None
        h, u = _out_proj(merged, w_out_b, layer, h, g_next, tm=tm)

    out = _final_norm(h.reshape(batch, lp // CHUNK, CHUNK, d), final_norm_g[None, :],
                      nc=NORM_CHUNKS)
    return out.reshape(batch, seq, d)
```

```python
import functools

import numpy as np
import jax
import jax.numpy as jnp
from jax import lax
from jax.experimental import pallas as pl
from jax.experimental.pallas import tpu as pltpu

F32 = jnp.float32
BF16 = jnp.bfloat16

N_META = 16
RET_HEADS = 8
CONV_K = 3
CHUNK = 128
ROPE_BASE = 10000.0
EPS = 1e-6

V7X_LANES = 128
V7X_F32_SUBLANES = 8
V7X_BF16_SUBLANES = 16
V7X_VMEM_BYTES = 64 * 1024 * 1024
V7X_MXU_COLS = 256

PROJ_ROW_TARGET = 1376
ROW_TILE_TARGET = 768
COL_TILE = 2048
GATE_PIECE = V7X_MXU_COLS
CAST_ROWS = 32
NORM_CHUNKS = 8
VMEM_LIMIT = V7X_VMEM_BYTES - 8 * 1024 * 1024


def _row_tile(m, target, quantum):
    best = quantum
    for t in range(quantum, target + 1, quantum):
        if m % t == 0:
            best = t
    return best


def _params(*sem):
    return pltpu.CompilerParams(dimension_semantics=sem, vmem_limit_bytes=VMEM_LIMIT)


def _rmsnorm(x, g):
    ms = jnp.mean(x * x, axis=-1, keepdims=True)
    return (x * lax.rsqrt(ms + EPS)) * g


def _embed_kernel(xa_ref, xb_ref, meta_ref, g_ref, h_ref, u_ref, *, front, nc):
    g = g_ref[...]
    d = g.shape[1]
    lead = jnp.concatenate([jnp.zeros((front, d), F32), meta_ref[...]], axis=0)
    first = jnp.where(pl.program_id(1) == 0, lead, xb_ref[0, 0])
    h_ref[0, 0] = first
    u_ref[0, 0] = _rmsnorm(first, g).astype(BF16)
    for c in range(1, nc):
        rows = xa_ref[0, c - 1]
        h_ref[0, c] = rows
        u_ref[0, c] = _rmsnorm(rows, g).astype(BF16)


def _embed_tail_kernel(x_ref, g_ref, h_in_ref, u_in_ref, h_ref, u_ref):
    del h_in_ref, u_in_ref
    rows = x_ref[0, 0]
    h_ref[0, 0] = rows
    u_ref[0, 0] = _rmsnorm(rows, g_ref[...]).astype(BF16)


def _embed(x, meta, g, *, front, nc):
    batch, seq, d = x.shape
    n_in = seq // CHUNK
    x4 = x.reshape(batch, n_in, CHUNK, d)
    shapes = [jax.ShapeDtypeStruct((batch, n_in + 1, CHUNK, d), F32),
              jax.ShapeDtypeStruct((batch, n_in + 1, CHUNK, d), BF16)]
    block = lambda n: (1, n, CHUNK, d)
    kern = functools.partial(_embed_kernel, front=front, nc=nc)
    h4, u4 = pl.pallas_call(
        kern,
        grid=(batch, n_in // nc),
        in_specs=[
            pl.BlockSpec(block(nc), lambda b, t: (b, t, 0, 0)),
            pl.BlockSpec(block(1), lambda b, t: (b, jnp.maximum(nc * t - 1, 0), 0, 0)),
            pl.BlockSpec((N_META, d), lambda b, t: (0, 0)),
            pl.BlockSpec((1, d), lambda b, t: (0, 0)),
        ],
        out_specs=[pl.BlockSpec(block(nc), lambda b, t: (b, t, 0, 0)),
                   pl.BlockSpec(block(nc), lambda b, t: (b, t, 0, 0))],
        out_shape=shapes,
        compiler_params=_params("parallel", "parallel"),
        name="embed",
    )(x4, x4, meta, g)
    h4, u4 = pl.pallas_call(
        _embed_tail_kernel,
        grid=(batch,),
        in_specs=[
            pl.BlockSpec(block(1), lambda b: (b, n_in - 1, 0, 0)),
            pl.BlockSpec((1, d), lambda b: (0, 0)),
            pl.BlockSpec(memory_space=pl.ANY),
            pl.BlockSpec(memory_space=pl.ANY),
        ],
        out_specs=[pl.BlockSpec(block(1), lambda b: (b, n_in, 0, 0)),
                   pl.BlockSpec(block(1), lambda b: (b, n_in, 0, 0))],
        out_shape=shapes,
        input_output_aliases={2: 0, 3: 1},
        compiler_params=_params("parallel"),
        name="embed_tail",
    )(x4, g, h4, u4)
    m = batch * (n_in + 1) * CHUNK
    return h4.reshape(m, d), u4.reshape(m, d)


def _proj_kernel(*refs, kind, dk, q_groups, q_scale):
    u_ref, w_ref, o_ref = refs[0], refs[1], refs[-1]
    acc = jnp.dot(u_ref[...], w_ref[...], preferred_element_type=F32)
    if kind == "rotary":
        cs = refs[2][...]
        sn = refs[3][...]
        for gidx in range(acc.shape[1] // dk):
            cols = slice(gidx * dk, (gidx + 1) * dk)
            x = acc[:, cols]
            y = x * cs + pltpu.roll(x, dk // 2, 1) * sn
            if gidx < q_groups:
                y = y * q_scale
            o_ref[:, cols] = y.astype(BF16)
    elif kind == "plain":
        o_ref[...] = acc.astype(BF16)
    else:
        assert kind == "gain_silu"
        o_ref[...] = (jax.nn.silu(acc) * refs[2][...]).astype(BF16)


def _proj(u, w_all, layer, col0, n_cols, kind, extras=(), *, tm, dk=0, q_groups=0):
    m, d = u.shape
    tn = COL_TILE
    assert col0 % tn == 0 and n_cols % tn == 0
    c0 = col0 // tn
    extra_specs = []
    for e in extras:
        if e.shape[0] == 1:
            extra_specs.append(pl.BlockSpec((1, tn), lambda i, j: (0, j)))
        else:
            tiles_per_seq = e.shape[0] // tm
            extra_specs.append(pl.BlockSpec((tm, e.shape[1]),
                                            lambda i, j, t=tiles_per_seq: (i % t, 0)))
    kern = functools.partial(_proj_kernel, kind=kind, dk=dk, q_groups=q_groups,
                             q_scale=dk ** -0.5 if dk else 1.0)
    return pl.pallas_call(
        kern,
        grid=(m // tm, n_cols // tn),
        in_specs=[
            pl.BlockSpec((tm, d), lambda i, j: (i, 0)),
            pl.BlockSpec((None, d, tn), lambda i, j: (layer, 0, c0 + j)),
        ] + extra_specs,
        out_specs=pl.BlockSpec((tm, tn), lambda i, j: (i, j)),
        out_shape=jax.ShapeDtypeStruct((m, n_cols), BF16),
        compiler_params=_params("parallel", "arbitrary"),
        name="proj_" + kind,
    )(u, w_all, *extras)


def _proj_conv_kernel(u_ref, wx_ref, wpre_ref, wpost_ref, wg_ref, cw_ref, cb_ref, o_ref,
                      p_ref, carry_ref, *, tm, tc):
    halo = V7X_F32_SUBLANES

    @pl.when(pl.program_id(1) == 0)
    def _():
        carry_ref[...] = jnp.zeros((halo, tc), F32)

    def proj(w_ref):
        return jnp.dot(u_ref[...], w_ref[...], preferred_element_type=F32)

    p = proj(wpre_ref) * proj(wx_ref)
    p_ref[0:halo, :] = carry_ref[...]
    p_ref[halo:halo + tm, :] = p
    carry_ref[...] = p[tm - halo:tm, :]
    z = (cw_ref[0:1, :] * p_ref[halo - 2:halo - 2 + tm, :]
         + cw_ref[1:2, :] * p_ref[halo - 1:halo - 1 + tm, :]
         + cw_ref[2:3, :] * p) + cb_ref[...]
    gate = jax.nn.silu(proj(wg_ref))
    o_ref[...] = ((proj(wpost_ref) * z) * gate).astype(BF16)


def _proj_conv(u, w_all, layer, col0, c_w, conv_w, conv_b, *, tm):
    m, d = u.shape
    tc = COL_TILE // 4
    nj = c_w // tc
    assert col0 % tc == 0 and c_w % tc == 0

    def w_spec(section):
        c0 = (col0 + section * c_w) // tc
        return pl.BlockSpec((None, d, tc), lambda j, i: (layer, 0, c0 + j))

    kern = functools.partial(_proj_conv_kernel, tm=tm, tc=tc)
    return pl.pallas_call(
        kern,
        grid=(nj, m // tm),
        in_specs=[
            pl.BlockSpec((tm, d), lambda j, i: (i, 0)),
            w_spec(0), w_spec(1), w_spec(2), w_spec(3),
            pl.BlockSpec((CONV_K, tc), lambda j, i: (0, j)),
            pl.BlockSpec((1, tc), lambda j, i: (0, j)),
        ],
        out_specs=pl.BlockSpec((tm, tc), lambda j, i: (i, j)),
        out_shape=jax.ShapeDtypeStruct((m, c_w), BF16),
        scratch_shapes=[pltpu.VMEM((tm + V7X_F32_SUBLANES, tc), F32),
                        pltpu.VMEM((V7X_F32_SUBLANES, tc), F32)],
        compiler_params=_params("arbitrary", "arbitrary"),
        name="proj_conv",
    )(u, w_all, w_all, w_all, w_all, conv_w, conv_b)


def _gate_ret_kernel(cd_ref, u_ref, w_ref, q_ref, k_ref, v_ref, sg_ref, dec_ref, qd_ref, kd_ref,
                     *rest, tm, hp, dk, dv, chunks_per_seq, cast_next):
    if cast_next:
        wsrc_ref, g_ref, yr_ref, wdst_ref, s_ref = rest
        wdst_ref[...] = wsrc_ref[...].astype(BF16)
    else:
        g_ref, yr_ref, s_ref = rest
    j = pl.program_id(0)
    i = pl.program_id(1)

    @pl.when(i == 0)
    def _():
        s_ref[...] = jnp.zeros_like(s_ref)

    nt = (((1,), (1,)), ((), ()))
    tn = (((0,), (0,)), ((), ()))
    n_chunks = tm // CHUNK
    n_pieces = w_ref.shape[1] // GATE_PIECE
    assert n_pieces > n_chunks
    hs = range(hp)
    kcs = [slice(hd * dk, (hd + 1) * dk) for hd in hs]
    vcs = [slice(hd * dv, (hd + 1) * dv) for hd in hs]

    def gate_piece(pi):
        cols = slice(pi * GATE_PIECE, (pi + 1) * GATE_PIECE)
        acc = jnp.dot(u_ref[...], w_ref[:, cols], preferred_element_type=F32)
        g_ref[:, cols] = jax.nn.sigmoid(acc).astype(BF16)

    def stage_a(c):
        r = slice(c * CHUNK, (c + 1) * CHUNK)
        keep = jnp.where((i * n_chunks + c) % chunks_per_seq == 0, 0.0, 1.0).astype(F32)
        st = [s_ref[hd] * keep for hd in hs]
        scores = [lax.dot_general(q_ref[r, kcs[hd]], k_ref[r, kcs[hd]], nt,
                                  preferred_element_type=F32) for hd in hs]
        rhs = [jnp.concatenate([v_ref[r, vcs[hd]], st[hd].astype(BF16)], axis=0) for hd in hs]
        for hd in hs:
            kdec = (k_ref[r, kcs[hd]].astype(F32) * kd_ref[hd]).astype(BF16)
            s_ref[hd] = st[hd] * cd_ref[j * hp + hd] + lax.dot_general(
                kdec, v_ref[r, vcs[hd]], tn, preferred_element_type=F32)
        return c, scores, rhs

    def stage_b(c, scores, rhs):
        r = slice(c * CHUNK, (c + 1) * CHUNK)
        for hd in hs:
            lhs = jnp.concatenate(
                [(scores[hd] * dec_ref[hd]).astype(BF16),
                 (q_ref[r, kcs[hd]].astype(F32) * qd_ref[hd]).astype(BF16)], axis=1)
            y = jnp.dot(lhs, rhs[hd], preferred_element_type=F32)
            mu = jnp.mean(y, axis=-1, keepdims=True)
            yc = y - mu
            var = jnp.mean(yc * yc, axis=-1, keepdims=True)
            yn = yc * lax.rsqrt(var + EPS)
            yr_ref[r, vcs[hd]] = (yn * sg_ref[r, vcs[hd]].astype(F32)).astype(BF16)

    pending = None
    for pi in range(n_pieces):
        if pending is not None:
            stage_b(*pending)
        pending = stage_a(pi) if pi < n_chunks else None
        gate_piece(pi)


def _gate_ret(u, w_all, layer, col0, n_cols, qk, v, sg, chunk_decay, decay, q_decay, k_decay,
              w_next_src, next_layer, *, tm, lp, heads, dk, dv):
    m, d = u.shape
    tn = COL_TILE
    nj = n_cols // tn
    ni = m // tm
    hp = heads // nj
    assert col0 % tn == 0 and heads % nj == 0 and tm % CHUNK == 0 and lp % CHUNK == 0
    c0 = col0 // tn
    cast_next = w_next_src is not None
    kern = functools.partial(_gate_ret_kernel, tm=tm, hp=hp, dk=dk, dv=dv,
                             chunks_per_seq=lp // CHUNK, cast_next=cast_next)
    in_specs = [
        pl.BlockSpec(memory_space=pltpu.SMEM),
        pl.BlockSpec((tm, d), lambda j, i: (i, 0)),
        pl.BlockSpec((None, d, tn), lambda j, i: (layer, 0, c0 + j)),
        pl.BlockSpec((tm, hp * dk), lambda j, i: (i, j)),
        pl.BlockSpec((tm, hp * dk), lambda j, i: (i, nj + j)),
        pl.BlockSpec((tm, hp * dv), lambda j, i: (i, j)),
        pl.BlockSpec((tm, hp * dv), lambda j, i: (i, j)),
        pl.BlockSpec((hp, CHUNK, CHUNK), lambda j, i: (j, 0, 0)),
        pl.BlockSpec((hp, CHUNK, dk), lambda j, i: (j, 0, 0)),
        pl.BlockSpec((hp, CHUNK, dk), lambda j, i: (j, 0, 0)),
    ]
    out_specs = [pl.BlockSpec((tm, tn), lambda j, i: (i, j)),
                 pl.BlockSpec((tm, hp * dv), lambda j, i: (i, j))]
    out_shape = [jax.ShapeDtypeStruct((m, n_cols), BF16),
                 jax.ShapeDtypeStruct((m, heads * dv), BF16)]
    args = [chunk_decay, u, w_all, qk, qk, v, sg, decay, q_decay, k_decay]
    if cast_next:
        n_all = w_next_src.shape[-1]
        rows = next(r for r in range(CAST_ROWS, d + 1, CAST_ROWS)
                    if d % r == 0 and d // r <= nj * ni)
        n_slabs = d // rows
        slab = lambda j, i: jnp.minimum(j * ni + i, n_slabs - 1)
        in_specs.append(pl.BlockSpec((None, rows, n_all),
                                     lambda j, i: (next_layer, slab(j, i), 0)))
        out_specs.append(pl.BlockSpec((None, rows, n_all), lambda j, i: (0, slab(j, i), 0)))
        out_shape.append(jax.ShapeDtypeStruct((1, d, n_all), BF16))
        args.append(w_next_src)
    outs = pl.pallas_call(
        kern,
        grid=(nj, ni),
        in_specs=in_specs,
        out_specs=out_specs,
        out_shape=out_shape,
        scratch_shapes=[pltpu.VMEM((hp, dk, dv), F32)],
        compiler_params=_params("arbitrary", "arbitrary"),
        name="gate_ret",
    )(*args)
    return (outs[0], outs[1], outs[2]) if cast_next else (outs[0], outs[1], None)


def _branch_kernel(yr_ref, yc_ref, w0_ref, w1_ref, mr_ref, mc_ref, o_ref):
    a = jnp.dot(yr_ref[...], w0_ref[...], preferred_element_type=F32)
    b = jnp.dot(yc_ref[...], w1_ref[...], preferred_element_type=F32)
    o_ref[...] = (mr_ref[...].astype(F32) * a + mc_ref[...].astype(F32) * b).astype(BF16)


def _branch(yr, yc, w_all, layer, gates, *, tm):
    m, kdim = yr.shape
    n = w_all.shape[-1]
    tn = COL_TILE
    nj = n // tn
    return pl.pallas_call(
        _branch_kernel,
        grid=(m // tm, nj),
        in_specs=[
            pl.BlockSpec((tm, kdim), lambda i, j: (i, 0)),
            pl.BlockSpec((tm, kdim), lambda i, j: (i, 0)),
            pl.BlockSpec((None, None, kdim, tn), lambda i, j: (layer, 0, 0, j)),
            pl.BlockSpec((None, None, kdim, tn), lambda i, j: (layer, 1, 0, j)),
            pl.BlockSpec((tm, tn), lambda i, j: (i, j)),
            pl.BlockSpec((tm, tn), lambda i, j: (i, nj + j)),
        ],
        out_specs=pl.BlockSpec((tm, tn), lambda i, j: (i, j)),
        out_shape=jax.ShapeDtypeStruct((m, n), BF16),
        compiler_params=_params("parallel", "arbitrary"),
        name="branch",
    )(yr, yc, w_all, w_all, gates, gates)


def _out_kernel(a_ref, w_ref, h_ref, o_ref):
    o_ref[...] = h_ref[...] + jnp.dot(a_ref[...], w_ref[...], preferred_element_type=F32)


def _out_norm_kernel(a_ref, w_ref, h_ref, g_ref, o_ref, u_ref):
    hn = h_ref[...] + jnp.dot(a_ref[...], w_ref[...], preferred_element_type=F32)
    o_ref[...] = hn
    u_ref[...] = _rmsnorm(hn, g_ref[...]).astype(BF16)


def _out_proj(a, w_all, layer, h, g_next, *, tm):
    m, kdim = a.shape
    n = w_all.shape[-1]
    row_spec = lambda width: pl.BlockSpec((tm, width), lambda i: (i, 0))
    in_specs = [row_spec(kdim),
                pl.BlockSpec((None, kdim, n), lambda i: (layer, 0, 0),
                             pipeline_mode=pl.Buffered(1)),
                row_spec(n)]
    if g_next is None:
        return pl.pallas_call(
            _out_kernel, grid=(m // tm,), in_specs=in_specs, out_specs=row_spec(n),
            out_shape=jax.ShapeDtypeStruct((m, n), F32),
            compiler_params=_params("parallel"), name="out_proj",
        )(a, w_all, h), None
    return pl.pallas_call(
        _out_norm_kernel, grid=(m // tm,),
        in_specs=in_specs + [pl.BlockSpec((1, n), lambda i: (0, 0))],
        out_specs=[row_spec(n), row_spec(n)],
        out_shape=[jax.ShapeDtypeStruct((m, n), F32), jax.ShapeDtypeStruct((m, n), BF16)],
        compiler_params=_params("parallel"), name="out_proj_norm",
    )(a, w_all, h, g_next)


def _final_norm_kernel(a_ref, b_ref, g_ref, o_ref, *, nc):
    g = g_ref[...]
    for c in range(nc - 1):
        o_ref[0, c] = _rmsnorm(a_ref[0, c + 1], g)
    o_ref[0, nc - 1] = _rmsnorm(b_ref[0, 0], g)


def _final_norm(h4, g, *, nc):
    b, n_chunks, _, d = h4.shape
    kern = functools.partial(_final_norm_kernel, nc=nc)
    return pl.pallas_call(
        kern,
        grid=(b, (n_chunks - 1) // nc),
        in_specs=[
            pl.BlockSpec((1, nc, CHUNK, d), lambda bi, t: (bi, t, 0, 0)),
            pl.BlockSpec((1, 1, CHUNK, d), lambda bi, t: (bi, nc * t + nc, 0, 0)),
            pl.BlockSpec((1, d), lambda bi, t: (0, 0)),
        ],
        out_specs=pl.BlockSpec((1, nc, CHUNK, d), lambda bi, t: (bi, t, 0, 0)),
        out_shape=jax.ShapeDtypeStruct((b, n_chunks - 1, CHUNK, d), F32),
        compiler_params=_params("parallel", "parallel"),
        name="final_norm",
    )(h4, h4, g)


def _tables(lp, front, dk, heads):
    f32 = np.float32
    half = dk // 2
    inv_freq = f32(ROPE_BASE) ** (-np.arange(half, dtype=f32) / f32(half))
    pos = np.maximum(np.arange(lp, dtype=np.int32) - front, 0).astype(f32)
    ang = pos[:, None] * inv_freq[None, :]
    cos = np.cos(ang).astype(f32)
    sin = np.sin(ang).astype(f32)
    cosf = np.concatenate([cos, cos], axis=-1)
    sinf = np.concatenate([-sin, sin], axis=-1)

    log_g = np.log(f32(1.0) - np.exp2(f32(-5.0) - np.arange(heads, dtype=f32))).astype(f32)
    idx = np.arange(CHUNK, dtype=f32)
    diff = idx[:, None] - idx[None, :]
    decay = np.where(diff >= 0, np.exp(log_g[:, None, None] * np.maximum(diff, f32(0.0))),
                     f32(0.0)).astype(f32)
    q_decay = np.broadcast_to(np.exp(log_g[:, None] * (idx + f32(1.0)))[:, :, None],
                              (heads, CHUNK, dk)).astype(f32)
    k_decay = np.broadcast_to(np.exp(log_g[:, None] * (f32(CHUNK - 1.0) - idx))[:, :, None],
                              (heads, CHUNK, dk)).astype(f32)
    chunk_decay = np.exp(log_g * f32(CHUNK)).astype(f32)
    return cosf, sinf, decay, q_decay, k_decay, chunk_decay


def kernel(x, meta_tokens, norm_g, w_in, conv_w, conv_b, gn_g, w_branch, w_out, final_norm_g):
    batch, seq, d = x.shape
    depth = w_in.shape[0]
    heads = RET_HEADS
    dk = d // 16
    dv = 2 * dk
    qk_w = heads * dk
    v_w = heads * dv
    c_w = d
    assert dk == V7X_LANES and seq % (CHUNK * NORM_CHUNKS) == 0

    front = CHUNK - N_META
    lp = front + N_META + seq
    m = batch * lp
    tm_proj = _row_tile(lp, PROJ_ROW_TARGET, V7X_BF16_SUBLANES)
    tm = _row_tile(m, ROW_TILE_TARGET, CHUNK)

    cosf, sinf, decay, q_decay, k_decay, chunk_decay = _tables(lp, front, dk, heads)

    o_v = 2 * qk_w
    o_gr = o_v + v_w
    o_cx = o_gr + v_w
    o_mr = o_cx + 4 * c_w

    w_layer = w_in[0:1].astype(BF16)
    w_branch_b = w_branch.astype(BF16)
    w_out_b = w_out.astype(BF16)

    h, u = _embed(x, meta_tokens.astype(x.dtype), norm_g[0][None, :], front=front,
                  nc=NORM_CHUNKS)

    for layer in range(depth):
        proj = functools.partial(_proj, u, w_layer, 0, tm=tm_proj)
        qk = proj(0, 2 * qk_w, "rotary", (cosf, sinf), dk=dk, q_groups=heads)
        v = proj(o_v, v_w, "plain")
        sg = proj(o_gr, v_w, "gain_silu", (gn_g[layer][None, :],))
        yc = _proj_conv(u, w_layer, 0, o_cx, c_w, conv_w[layer], conv_b[layer][None, :],
                        tm=tm_proj)
        has_next = layer + 1 < depth
        gates, yr, w_next = _gate_ret(u, w_layer, 0, o_mr, 2 * d, qk, v, sg, chunk_decay, decay,
                                      q_decay, k_decay, w_in if has_next else None, layer + 1,
                                      tm=tm, lp=lp, heads=heads, dk=dk, dv=dv)
        w_layer = w_next
        merged = _branch(yr, yc, w_branch_b, layer, gates, tm=tm)
        g_next = norm_g[layer + 1][None, :] if layer + 1 < depth else None
        h, u = _out_proj(merged, w_out_b, layer, h, g_next, tm=tm)

    out = _final_norm(h.reshape(batch, lp // CHUNK, CHUNK, d), final_norm_g[None, :],
                      nc=NORM_CHUNKS)
    return out.reshape(batch, seq, d)
```
